```python
import numpy as np
import jax
import jax.numpy as jnp
from jax import lax

D_MODEL = 2048
BATCH = 4
SEQ = 2048
DEPTH = 2
DEC_BATCH = 128
DEC_SEQ = 4
PAST_LEN = 16384
PAGE_SIZE = 128

RET_HEADS = 4
RET_KEY_DIM = 64
RET_VAL_DIM = 128
RET_CHUNK = 128
MOBA_HEADS = 4
MOBA_KV_HEADS = 1
MOBA_HEAD_DIM = 128
MOBA_BLOCK = 256
MOBA_TOPK = 3
MOBA_Q_BLOCK = 64
MLA_HEADS = 4
MLA_Q_RANK = 384
MLA_KV_RANK = 256
MLA_NOPE_DIM = 128
MLA_ROPE_DIM = 64
MLA_V_DIM = 128
MLA_Q_BLOCK = 128
GLA_HEADS = 4
GLA_KEY_DIM = 64
GLA_VAL_DIM = 128
GLA_GATE_RANK = 16
GLA_GATE_TAU = 16.0
GLA_CHUNK = 64
N_BRANCH = 4
BRANCH_WIDTH = RET_HEADS * RET_VAL_DIM
FFN_HIDDEN = ((8 * D_MODEL + 2) // 3 + 255) // 256 * 256
ROPE_THETA = 10000.0
DN_ALPHA = (2 * DEPTH) ** 0.25
DN_BETA = (8 * DEPTH) ** -0.25
LN_EPS = 1e-5
RMS_EPS = 1e-6
F32 = jnp.float32

IN_SPLITS = (
    RET_HEADS * RET_KEY_DIM, RET_HEADS * RET_KEY_DIM, RET_HEADS * RET_VAL_DIM, RET_HEADS * RET_VAL_DIM,
    MOBA_HEADS * MOBA_HEAD_DIM, MOBA_KV_HEADS * MOBA_HEAD_DIM, MOBA_KV_HEADS * MOBA_HEAD_DIM,
    MLA_Q_RANK, MLA_KV_RANK, MLA_ROPE_DIM,
    GLA_HEADS * GLA_KEY_DIM, GLA_HEADS * GLA_KEY_DIM, GLA_HEADS * GLA_VAL_DIM, GLA_GATE_RANK,
    GLA_HEADS * GLA_VAL_DIM,
    N_BRANCH * D_MODEL,
)
IN_COLS = sum(IN_SPLITS)
VALUE_SLOTS = (2, 6, 12)

kernel_name = 'hybrid_retention_moba_mla_gla_step'


def _split(t, sizes):
    return jnp.split(t, np.cumsum(sizes)[:-1].tolist(), axis=-1)


def _layernorm(x, g, b):
    xf = x.astype(F32)
    mu = jnp.mean(xf, -1, keepdims=True)
    var = jnp.mean(jnp.square(xf - mu), -1, keepdims=True)
    return ((xf - mu) * lax.rsqrt(var + LN_EPS) * g + b).astype(x.dtype)


def _rmsnorm(x, g):
    xf = x.astype(F32)
    return (xf * lax.rsqrt(jnp.mean(xf * xf, -1, keepdims=True) + RMS_EPS) * g).astype(x.dtype)


def _rope(x, pos, inv_freq):
    ang = pos.astype(F32)[:, None] * inv_freq[None, :]
    shape = (1, x.shape[1]) + (1,) * (x.ndim - 3) + (inv_freq.shape[0],)
    cos = jnp.cos(ang).reshape(shape).astype(x.dtype)
    sin = jnp.sin(ang).reshape(shape).astype(x.dtype)
    x1, x2 = jnp.split(x, 2, axis=-1)
    return jnp.concatenate([x1 * cos - x2 * sin, x2 * cos + x1 * sin], axis=-1)


def _to_chunks(t, chunk):
    B, S, H, d = t.shape
    return t.astype(F32).reshape(B, S // chunk, chunk, H, d).transpose(1, 0, 3, 2, 4)


def _from_chunks(o, dtype):
    n, B, H, C, d = o.shape
    return o.transpose(1, 0, 3, 2, 4).reshape(B, n * C, H, d).astype(dtype)


def _retention(q, k, v, s0, chunk):
    H = q.shape[2]
    log_g = jnp.log1p(-jnp.exp2(-5.0 - jnp.arange(H, dtype=F32)))
    idx = jnp.arange(chunk, dtype=F32)
    diff = idx[:, None] - idx[None, :]
    decay = jnp.where(diff >= 0, jnp.exp(jnp.maximum(diff, 0.0)[None] * log_g[:, None, None]), 0.0)
    q_dec = jnp.exp((idx + 1.0)[None, :] * log_g[:, None])[..., None]
    k_dec = jnp.exp((chunk - 1.0 - idx)[None, :] * log_g[:, None])[..., None]
    c_dec = jnp.exp(chunk * log_g)[:, None, None]

    def step(s, inp):
        qc, kc, vc = inp
        att = jnp.einsum('bhid,bhjd->bhij', qc, kc) * decay
        o = jnp.einsum('bhij,bhjv->bhiv', att, vc) + jnp.einsum('bhid,bhdv->bhiv', qc * q_dec, s)
        s = s * c_dec + jnp.einsum('bhjd,bhjv->bhdv', kc * k_dec, vc)
        return s, o

    s, o = lax.scan(step, s0.astype(F32), (_to_chunks(q, chunk), _to_chunks(k, chunk), _to_chunks(v, chunk)))
    return _from_chunks(o, v.dtype), s.astype(v.dtype)


def _gla(q, k, v, log_a, s0, chunk):
    mask = jnp.tril(jnp.ones((chunk, chunk), bool))[:, :, None]

    def step(s, inp):
        qc, kc, vc, ac = inp
        b = jnp.cumsum(ac, axis=2)
        rel = jnp.where(mask, jnp.exp(jnp.minimum(b[:, :, :, None, :] - b[:, :, None, :, :], 0.0)), 0.0)
        att = jnp.einsum('bhid,bhjd,bhijd->bhij', qc, kc, rel)
        b_last = b[:, :, -1:, :]
        o = jnp.einsum('bhij,bhjv->bhiv', att, vc) + jnp.einsum('bhid,bhdv->bhiv', qc * jnp.exp(b), s)
        s = s * jnp.exp(b_last)[:, :, 0, :, None] + jnp.einsum('bhjd,bhjv->bhdv', kc * jnp.exp(b_last - b), vc)
        return s, o

    s, o = lax.scan(step, s0.astype(F32), (_to_chunks(q, chunk), _to_chunks(k, chunk),
                                         _to_chunks(v, chunk), _to_chunks(log_a, chunk)))
    return _from_chunks(o, v.dtype), s.astype(v.dtype)


def _moba_prompt(q, k, v):
    B, S, Hq, dh = q.shape
    Hkv = k.shape[2]
    kvh = jnp.arange(Hq) // (Hq // Hkv)
    nb = -(-S // MOBA_BLOCK)
    pad = nb * MOBA_BLOCK - S
    kp = jnp.pad(k, ((0, 0), (0, pad), (0, 0), (0, 0)))
    vp = jnp.pad(v, ((0, 0), (0, pad), (0, 0), (0, 0)))
    kb = kp.reshape(B, nb, MOBA_BLOCK, Hkv, dh)
    vb = vp.reshape(B, nb, MOBA_BLOCK, Hkv, dh)
    n_full = S // MOBA_BLOCK
    n_sel = min(MOBA_TOPK, (S - 1) // MOBA_BLOCK)
    scale = dh ** -0.5
    if n_sel > 0:
        means = jnp.mean(kb[:, :n_full].astype(F32), axis=2)[:, :, kvh]
    nq = S // MOBA_Q_BLOCK
    qb = q.reshape(B, nq, MOBA_Q_BLOCK, Hq, dh).transpose(1, 0, 3, 2, 4)
    bidx = jnp.arange(B)[:, None, None, None]
    kvh_idx = kvh[None, :, None, None]

    def block(args):
        qi, i = args
        q0 = i * MOBA_Q_BLOCK
        qpos = q0 + jnp.arange(MOBA_Q_BLOCK)
        own = q0 // MOBA_BLOCK
        own_start = own * MOBA_BLOCK
        k_own = lax.dynamic_slice_in_dim(kp, own_start, MOBA_BLOCK, axis=1)[:, :, kvh]
        v_own = lax.dynamic_slice_in_dim(vp, own_start, MOBA_BLOCK, axis=1)[:, :, kvh]
        kpos = own_start + jnp.arange(MOBA_BLOCK)
        s_own = jnp.einsum('bhqd,bkhd->bhqk', qi, k_own).astype(F32) * scale
        s_own = jnp.where(kpos[None, :] <= qpos[:, None], s_own, -jnp.inf)
        if n_sel == 0:
            p = jax.nn.softmax(s_own, axis=-1).astype(v.dtype)
            return jnp.einsum('bhqk,bkhd->bhqd', p, v_own)
        gate = jnp.einsum('bhqd,bnhd->bhqn', qi.astype(F32), means)
        gate = jnp.where(jnp.arange(n_full) < own, gate, -jnp.inf)
        _, sel = lax.top_k(gate, n_sel)
        valid = sel < own
        k_sel = kb[bidx, sel, :, kvh_idx]
        v_sel = vb[bidx, sel, :, kvh_idx]
        s_sel = jnp.einsum('bhqd,bhqnkd->bhqnk', qi, k_sel).astype(F32) * scale
        s_sel = jnp.where(valid[..., None], s_sel, -jnp.inf).reshape(B, Hq, MOBA_Q_BLOCK, n_sel * MOBA_BLOCK)
        p = jax.nn.softmax(jnp.concatenate([s_sel, s_own], axis=-1), axis=-1).astype(v.dtype)
        p_sel = p[..., :n_sel * MOBA_BLOCK].reshape(B, Hq, MOBA_Q_BLOCK, n_sel, MOBA_BLOCK)
        return (jnp.einsum('bhqnk,bhqnkd->bhqd', p_sel, v_sel)
                + jnp.einsum('bhqk,bkhd->bhqd', p[..., n_sel * MOBA_BLOCK:], v_own))

    o = lax.map(block, (qb, jnp.arange(nq)))
    return o.transpose(1, 0, 3, 2, 4).reshape(B, S, Hq * dh)


def _moba_sample(q, k_new, v_new, page_table, cache_k, cache_v, l):
    DB, T, Hq, dh = q.shape
    Hkv = k_new.shape[2]
    kvh = jnp.arange(Hq) // (Hq // Hkv)
    kvh_idx = kvh[:, None, None, None]
    ppb = MOBA_BLOCK // PAGE_SIZE
    n_full = PAST_LEN // MOBA_BLOCK
    n_sel = min(MOBA_TOPK, n_full)
    own_page0 = n_full * ppb
    n_own_pages = PAST_LEN // PAGE_SIZE - own_page0
    scale = dh ** -0.5
    causal = jnp.tril(jnp.ones((T, T), bool))

    def per_seq(args):
        qs, kn, vn, pt = args
        scores = []
        if n_sel > 0:
            k_past = cache_k[l, pt[:own_page0]].astype(F32)
            means = jnp.mean(k_past.reshape(n_full, MOBA_BLOCK, Hkv, dh), axis=1)[:, kvh]
            gate = jnp.einsum('thd,nhd->htn', qs.astype(F32), means)
            _, sel = lax.top_k(gate, n_sel)
            pages = pt[sel[..., None] * ppb + jnp.arange(ppb)]
            k_sel = cache_k[l, pages, :, kvh_idx].reshape(Hq, T, n_sel * MOBA_BLOCK, dh)
            v_sel = cache_v[l, pages, :, kvh_idx].reshape(Hq, T, n_sel * MOBA_BLOCK, dh)
            scores.append(jnp.einsum('thd,htkd->htk', qs, k_sel).astype(F32) * scale)
        if n_own_pages > 0:
            own = pt[own_page0:]
            k_own = cache_k[l, own].reshape(n_own_pages * PAGE_SIZE, Hkv, dh)[:, kvh]
            v_own = cache_v[l, own].reshape(n_own_pages * PAGE_SIZE, Hkv, dh)[:, kvh]
            scores.append(jnp.einsum('thd,khd->htk', qs, k_own).astype(F32) * scale)
        s_new = jnp.einsum('thd,khd->htk', qs, kn[:, kvh]).astype(F32) * scale
        scores.append(jnp.where(causal[None], s_new, -jnp.inf))
        p = jax.nn.softmax(jnp.concatenate(scores, axis=-1), axis=-1).astype(vn.dtype)
        o = jnp.einsum('htk,khd->thd', p[..., p.shape[-1] - T:], vn[:, kvh])
        off = 0
        if n_sel > 0:
            o = o + jnp.einsum('htk,htkd->thd', p[..., :n_sel * MOBA_BLOCK], v_sel)
            off = n_sel * MOBA_BLOCK
        if n_own_pages > 0:
            o = o + jnp.einsum('htk,khd->thd', p[..., off:off + n_own_pages * PAGE_SIZE], v_own)
        return o

    o = lax.map(per_seq, (q, k_new, v_new, page_table))
    return o.reshape(DB, T, Hq * dh)


def _mla_prompt(q_lat, q_rope, c, kr):
    B, S, H, R = q_lat.shape
    dr = q_rope.shape[-1]
    nq = S // MLA_Q_BLOCK
    scale = (MLA_NOPE_DIM + MLA_ROPE_DIM) ** -0.5
    kpos = jnp.arange(S)
    qlb = q_lat.reshape(B, nq, MLA_Q_BLOCK, H, R).transpose(1, 0, 2, 3, 4)
    qrb = q_rope.reshape(B, nq, MLA_Q_BLOCK, H, dr).transpose(1, 0, 2, 3, 4)

    def block(args):
        ql, qr, i = args
        qpos = i * MLA_Q_BLOCK + jnp.arange(MLA_Q_BLOCK)
        s = (jnp.einsum('bqhr,bkr->bhqk', ql, c) + jnp.einsum('bqhe,bke->bhqk', qr, kr)).astype(F32) * scale
        s = jnp.where(kpos[None, :] <= qpos[:, None], s, -jnp.inf)
        p = jax.nn.softmax(s, axis=-1).astype(c.dtype)
        return jnp.einsum('bhqk,bkr->bqhr', p, c)

    o = lax.map(block, (qlb, qrb, jnp.arange(nq)))
    return o.transpose(1, 0, 2, 3, 4).reshape(B, S, H, R)


def _mla_sample(q_lat, q_rope, c_new, kr_new, page_table, cache_lat, cache_rope, l):
    DB, T, H, R = q_lat.shape
    dr = q_rope.shape[-1]
    scale = (MLA_NOPE_DIM + MLA_ROPE_DIM) ** -0.5
    causal = jnp.tril(jnp.ones((T, T), bool))

    def per_seq(args):
        ql, qr, cn, kn, pt = args
        c_past = cache_lat[l, pt].reshape(PAST_LEN, R)
        kr_past = cache_rope[l, pt].reshape(PAST_LEN, dr)
        s_past = (jnp.einsum('thr,kr->htk', ql, c_past) + jnp.einsum('the,ke->htk', qr, kr_past)).astype(F32) * scale
        s_new = (jnp.einsum('thr,kr->htk', ql, cn) + jnp.einsum('the,ke->htk', qr, kn)).astype(F32) * scale
        s_new = jnp.where(causal[None], s_new, -jnp.inf)
        p = jax.nn.softmax(jnp.concatenate([s_past, s_new], axis=-1), axis=-1).astype(c_past.dtype)
        return (jnp.einsum('htk,kr->thr', p[..., :PAST_LEN], c_past)
                + jnp.einsum('htk,kr->thr', p[..., PAST_LEN:], cn))

    return lax.map(per_seq, (q_lat, q_rope, c_new, kr_new, page_table))


def _project(x, pos, l, W):
    B, S, _ = x.shape
    proj = jnp.einsum('bsd,dc->bsc', x, W['w_in'][l])
    (rq, rk, rv, rg, mq, mk, mv, cq, ckv, kr, gq, gk, gv, ga, gr, gt) = _split(proj, IN_SPLITS)
    ret_inv = 1.0 / (ROPE_THETA ** jnp.linspace(0.0, 1.0, RET_KEY_DIM // 2, dtype=F32))
    mla_inv = 1.0 / (ROPE_THETA ** (jnp.arange(0, MLA_ROPE_DIM, 2, dtype=F32) / MLA_ROPE_DIM))
    p = {}
    p['ret_q'] = _rope(rq.reshape(B, S, RET_HEADS, RET_KEY_DIM), pos, ret_inv)
    p['ret_k'] = _rope(rk.reshape(B, S, RET_HEADS, RET_KEY_DIM), pos, ret_inv) * (RET_KEY_DIM ** -0.5)
    p['ret_v'] = rv.reshape(B, S, RET_HEADS, RET_VAL_DIM)
    p['ret_g'] = rg
    p['moba_q'] = mq.reshape(B, S, MOBA_HEADS, MOBA_HEAD_DIM)
    p['moba_k'] = mk.reshape(B, S, MOBA_KV_HEADS, MOBA_HEAD_DIM)
    p['moba_v'] = mv.reshape(B, S, MOBA_KV_HEADS, MOBA_HEAD_DIM)
    qm = jnp.einsum('bsr,rhe->bshe', _rmsnorm(cq, W['mla_q_norm_g'][l]), W['mla_w_uq'][l])
    p['mla_q_lat'] = jnp.einsum('bshn,rhn->bshr', qm[..., :MLA_NOPE_DIM], W['mla_w_uk'][l])
    p['mla_q_rope'] = _rope(qm[..., MLA_NOPE_DIM:], pos, mla_inv)
    p['mla_c'] = _rmsnorm(ckv, W['mla_kv_norm_g'][l])
    p['mla_kr'] = _rope(kr, pos, mla_inv)
    p['gla_q'] = gq.reshape(B, S, GLA_HEADS, GLA_KEY_DIM) * (GLA_KEY_DIM ** -0.5)
    p['gla_k'] = gk.reshape(B, S, GLA_HEADS, GLA_KEY_DIM)
    p['gla_v'] = gv.reshape(B, S, GLA_HEADS, GLA_VAL_DIM)
    a_logit = jnp.einsum('bsr,rhk->bshk', ga, W['gla_w_a'][l]).astype(F32) + W['gla_b_a'][l]
    p['gla_log_a'] = jax.nn.log_sigmoid(a_logit) / GLA_GATE_TAU
    p['gla_r'] = gr
    p['gates'] = jax.nn.sigmoid(gt.reshape(B, S, N_BRANCH, D_MODEL) + W['b_gate'][l])
    return p


def _merge(p, l, W, ret_o, moba_o, mla_lat_o, gla_o):
    B, S = moba_o.shape[:2]
    ret = _rmsnorm(ret_o, W['ret_norm_g'][l]).reshape(B, S, BRANCH_WIDTH) * jax.nn.silu(p['ret_g'])
    mla = jnp.einsum('bshr,rhv->bshv', mla_lat_o, W['mla_w_uv'][l]).reshape(B, S, BRANCH_WIDTH)
    gla = _rmsnorm(gla_o, W['gla_norm_g'][l]).reshape(B, S, BRANCH_WIDTH) * jax.nn.silu(p['gla_r'])
    branches = jnp.stack([ret, moba_o, mla, gla], axis=2)
    y = jnp.einsum('bsnw,nwd->bsnd', branches, W['w_branch'][l])
    merged = jnp.sum(p['gates'] * y, axis=2)
    return jnp.einsum('bsd,de->bse', merged, W['w_out'][l])


def _ffn(x, l, W):
    g = jnp.einsum('bsd,df->bsf', x, W['ffn_w_gate'][l])
    u = jnp.einsum('bsd,df->bsf', x, W['ffn_w_up'][l])
    return jnp.einsum('bsf,fd->bsd', jax.nn.silu(g) * u, W['ffn_w_down'][l])


def _layer(x, pos, l, W, attend, ret_s0, gla_s0, ret_chunk, gla_chunk):
    p = _project(x, pos, l, W)
    ret_o, ret_s = _retention(p['ret_q'], p['ret_k'], p['ret_v'], ret_s0, ret_chunk)
    gla_o, gla_s = _gla(p['gla_q'], p['gla_k'], p['gla_v'], p['gla_log_a'], gla_s0, gla_chunk)
    moba_o, mla_lat_o = attend(p, l)
    mix = _merge(p, l, W, ret_o, moba_o, mla_lat_o, gla_o)
    x = _layernorm(DN_ALPHA * x + mix, W['ln1_g'][l], W['ln1_b'][l])
    x = _layernorm(DN_ALPHA * x + _ffn(x, l, W), W['ln2_g'][l], W['ln2_b'][l])
    return x, (p['moba_k'], p['moba_v'], p['mla_c'], p['mla_kr'], ret_s, gla_s)


def _stack(states, i):
    return jnp.stack([s[i] for s in states], axis=0)


def setup_inputs(seed: int = 0) -> dict:
    key = jax.random.key(seed)
    keys = iter(jax.random.split(key, 48))

    def nrm(shape, scale):
        r = jax.random.normal(next(keys), shape, F32)
        return r if scale == 1.0 else r * scale

    n_pages = PAST_LEN // PAGE_SIZE
    n_pool = (5 * DEC_BATCH * n_pages) // 4
    col_scale = np.concatenate([np.full((s,), DN_BETA if i in VALUE_SLOTS else 1.0, np.float32)
                                for i, s in enumerate(IN_SPLITS)])
    inp = {}
    inp['x_prompt'] = nrm((BATCH, SEQ, D_MODEL), 1.0)
    inp['x_sample'] = nrm((DEC_BATCH, DEC_SEQ, D_MODEL), 1.0)
    inp['cache_moba_k'] = nrm((DEPTH, n_pool, PAGE_SIZE, MOBA_KV_HEADS, MOBA_HEAD_DIM), 1.0)
    inp['cache_moba_v'] = nrm((DEPTH, n_pool, PAGE_SIZE, MOBA_KV_HEADS, MOBA_HEAD_DIM), 1.0)
    inp['cache_mla_latent'] = nrm((DEPTH, n_pool, PAGE_SIZE, MLA_KV_RANK), 1.0)
    inp['cache_mla_rope'] = nrm((DEPTH, n_pool, PAGE_SIZE, MLA_ROPE_DIM), 1.0)
    inp['state_retention'] = nrm((DEPTH, DEC_BATCH, RET_HEADS, RET_KEY_DIM, RET_VAL_DIM), 0.5)
    inp['state_gla'] = nrm((DEPTH, DEC_BATCH, GLA_HEADS, GLA_KEY_DIM, GLA_VAL_DIM), 0.5)
    perm = jax.random.permutation(next(keys), n_pool)
    inp['page_table'] = perm[:DEC_BATCH * n_pages].reshape(DEC_BATCH, n_pages).astype(jnp.int32)
    inp['w_in'] = nrm((DEPTH, D_MODEL, IN_COLS), D_MODEL ** -0.5) * jnp.asarray(col_scale)
    inp['b_gate'] = nrm((DEPTH, N_BRANCH, D_MODEL), 0.1)
    inp['ret_norm_g'] = 1.0 + nrm((DEPTH, RET_HEADS, RET_VAL_DIM), 0.02)
    inp['mla_q_norm_g'] = 1.0 + nrm((DEPTH, MLA_Q_RANK), 0.02)
    inp['mla_w_uq'] = nrm((DEPTH, MLA_Q_RANK, MLA_HEADS, MLA_NOPE_DIM + MLA_ROPE_DIM), MLA_Q_RANK ** -0.5)
    inp['mla_kv_norm_g'] = 1.0 + nrm((DEPTH, MLA_KV_RANK), 0.02)
    inp['mla_w_uk'] = nrm((DEPTH, MLA_KV_RANK, MLA_HEADS, MLA_NOPE_DIM), MLA_KV_RANK ** -0.5)
    inp['mla_w_uv'] = nrm((DEPTH, MLA_KV_RANK, MLA_HEADS, MLA_V_DIM), DN_BETA * MLA_KV_RANK ** -0.5)
    inp['gla_w_a'] = nrm((DEPTH, GLA_GATE_RANK, GLA_HEADS, GLA_KEY_DIM), GLA_GATE_RANK ** -0.5)
    inp['gla_b_a'] = nrm((DEPTH, GLA_HEADS, GLA_KEY_DIM), 0.1)
    inp['gla_norm_g'] = 1.0 + nrm((DEPTH, GLA_HEADS, GLA_VAL_DIM), 0.02)
    inp['w_branch'] = nrm((DEPTH, N_BRANCH, BRANCH_WIDTH, D_MODEL), DN_BETA * BRANCH_WIDTH ** -0.5)
    inp['w_out'] = nrm((DEPTH, D_MODEL, D_MODEL), DN_BETA * D_MODEL ** -0.5)
    inp['ln1_g'] = 1.0 + nrm((DEPTH, D_MODEL), 0.02)
    inp['ln1_b'] = nrm((DEPTH, D_MODEL), 0.02)
    inp['ffn_w_gate'] = nrm((DEPTH, D_MODEL, FFN_HIDDEN), DN_BETA * D_MODEL ** -0.5)
    inp['ffn_w_up'] = nrm((DEPTH, D_MODEL, FFN_HIDDEN), DN_BETA * D_MODEL ** -0.5)
    inp['ffn_w_down'] = nrm((DEPTH, FFN_HIDDEN, D_MODEL), DN_BETA * FFN_HIDDEN ** -0.5)
    inp['ln2_g'] = 1.0 + nrm((DEPTH, D_MODEL), 0.02)
    inp['ln2_b'] = nrm((DEPTH, D_MODEL), 0.02)
    return inp


def reference(x_prompt, x_sample, cache_moba_k, cache_moba_v, cache_mla_latent, cache_mla_rope,
              state_retention, state_gla, page_table,
              w_in, b_gate, ret_norm_g, mla_q_norm_g, mla_w_uq, mla_kv_norm_g, mla_w_uk, mla_w_uv,
              gla_w_a, gla_b_a, gla_norm_g, w_branch, w_out, ln1_g, ln1_b,
              ffn_w_gate, ffn_w_up, ffn_w_down, ln2_g, ln2_b):
    W = dict(w_in=w_in, b_gate=b_gate, ret_norm_g=ret_norm_g, mla_q_norm_g=mla_q_norm_g,
             mla_w_uq=mla_w_uq, mla_kv_norm_g=mla_kv_norm_g, mla_w_uk=mla_w_uk, mla_w_uv=mla_w_uv,
             gla_w_a=gla_w_a, gla_b_a=gla_b_a, gla_norm_g=gla_norm_g, w_branch=w_branch, w_out=w_out,
             ln1_g=ln1_g, ln1_b=ln1_b, ffn_w_gate=ffn_w_gate, ffn_w_up=ffn_w_up,
             ffn_w_down=ffn_w_down, ln2_g=ln2_g, ln2_b=ln2_b)
    pos_p = jnp.arange(SEQ, dtype=jnp.int32)
    pos_s = PAST_LEN + jnp.arange(DEC_SEQ, dtype=jnp.int32)
    ret0 = jnp.zeros((BATCH, RET_HEADS, RET_KEY_DIM, RET_VAL_DIM), F32)
    gla0 = jnp.zeros((BATCH, GLA_HEADS, GLA_KEY_DIM, GLA_VAL_DIM), F32)

    def prompt_attend(p, l):
        return (_moba_prompt(p['moba_q'], p['moba_k'], p['moba_v']),
                _mla_prompt(p['mla_q_lat'], p['mla_q_rope'], p['mla_c'], p['mla_kr']))

    def sample_attend(p, l):
        return (_moba_sample(p['moba_q'], p['moba_k'], p['moba_v'], page_table, cache_moba_k, cache_moba_v, l),
                _mla_sample(p['mla_q_lat'], p['mla_q_rope'], p['mla_c'], p['mla_kr'], page_table,
                            cache_mla_latent, cache_mla_rope, l))

    xp, xs = x_prompt, x_sample
    new_p, new_s = [], []
    for l in range(DEPTH):
        xp, st_p = _layer(xp, pos_p, l, W, prompt_attend, ret0, gla0, RET_CHUNK, GLA_CHUNK)
        xs, st_s = _layer(xs, pos_s, l, W, sample_attend, state_retention[l], state_gla[l], DEC_SEQ, DEC_SEQ)
        new_p.append(st_p)
        new_s.append(st_s)
    return (xp, xs,
            _stack(new_p, 0), _stack(new_p, 1), _stack(new_p, 2), _stack(new_p, 3), _stack(new_p, 4), _stack(new_p, 5),
            _stack(new_s, 0), _stack(new_s, 1), _stack(new_s, 2), _stack(new_s, 3), _stack(new_s, 4), _stack(new_s, 5))
```

```python
import functools

import numpy as np
import jax
import jax.numpy as jnp
from jax import lax
from jax.experimental import pallas as pl
from jax.experimental.pallas import tpu as pltpu

D_MODEL = 2048
BATCH = 4
SEQ = 2048
DEPTH = 2
DEC_BATCH = 128
DEC_SEQ = 4
PAST_LEN = 16384
PAGE_SIZE = 128
N_PAGES = PAST_LEN // PAGE_SIZE

RET_HEADS = 4
RET_KEY_DIM = 64
RET_VAL_DIM = 128
MOBA_HEADS = 4
MOBA_HEAD_DIM = 128
MOBA_BLOCK = 256
MOBA_TOPK = 3
MLA_HEADS = 4
MLA_Q_RANK = 384
MLA_KV_RANK = 256
MLA_NOPE_DIM = 128
MLA_ROPE_DIM = 64
MLA_V_DIM = 128
GLA_HEADS = 4
GLA_KEY_DIM = 64
GLA_VAL_DIM = 128
GLA_GATE_RANK = 16
GLA_GATE_TAU = 16.0
N_BRANCH = 4
BRANCH_WIDTH = 512
FFN_HIDDEN = 5632
ROPE_THETA = 10000.0
DN_ALPHA = (2 * DEPTH) ** 0.25
LN_EPS = 1e-5
RMS_EPS = 1e-6
F32 = jnp.float32
BF16 = jnp.bfloat16

N_HEADS = 4
MAIN_COLS = 4608
GATE_COL0 = 4560
MLA_SCALE = (MLA_NOPE_DIM + MLA_ROPE_DIM) ** -0.5
MOBA_SCALE = MOBA_HEAD_DIM ** -0.5

VMEM_LIMIT = 56 * 1024 * 1024
NEG_INF = float("-inf")


def _cparams(n_axes):
    return pltpu.CompilerParams(dimension_semantics=("arbitrary",) * n_axes, vmem_limit_bytes=VMEM_LIMIT)


def _dot(a, b):
    return jnp.dot(a, b, preferred_element_type=F32)


def _dot_nt(a, b):
    return lax.dot_general(a, b, (((1,), (1,)), ((), ())), preferred_element_type=F32)


def _dot_tn(a, b):
    return lax.dot_general(a, b, (((0,), (0,)), ((), ())), preferred_element_type=F32)


def _sigmoid(x):
    return 1.0 / (1.0 + jnp.exp(-x))


def _mm_kernel(a_ref, b_ref, o_ref):
    o_ref[...] = _dot(a_ref[...], b_ref[...]).astype(o_ref.dtype)


def _matmul(a, b, tm, tn, out_dtype):
    m, k = a.shape
    n = b.shape[1]
    return pl.pallas_call(
        _mm_kernel, grid=(m // tm, n // tn),
        in_specs=[pl.BlockSpec((tm, k), lambda i, j: (i, 0)), pl.BlockSpec((k, tn), lambda i, j: (0, j))],
        out_specs=pl.BlockSpec((tm, tn), lambda i, j: (i, j)),
        out_shape=jax.ShapeDtypeStruct((m, n), out_dtype),
        compiler_params=_cparams(2), name="mm")(a, b)


def _mm_heads_kernel(a_ref, b_ref, o_ref):
    o_ref[0] = _dot(a_ref[0], b_ref[0]).astype(o_ref.dtype)


def _matmul_heads(a, b, tm, out_dtype):
    h, m, k = a.shape
    n = b.shape[2]
    return pl.pallas_call(
        _mm_heads_kernel, grid=(h, m // tm),
        in_specs=[pl.BlockSpec((1, tm, k), lambda g, i: (g, i, 0)), pl.BlockSpec((1, k, n), lambda g, i: (g, 0, 0))],
        out_specs=pl.BlockSpec((1, tm, n), lambda g, i: (g, i, 0)),
        out_shape=jax.ShapeDtypeStruct((h, m, n), out_dtype),
        compiler_params=_cparams(2), name="mm_heads")(a, b)


def _ffn_up_kernel(x_ref, wg_ref, wu_ref, o_ref):
    x = x_ref[...]
    g = _dot(x, wg_ref[...])
    u = _dot(x, wu_ref[...])
    o_ref[...] = (g * _sigmoid(g) * u).astype(o_ref.dtype)


def _ffn_up(x, wg, wu, tm, tn):
    m, k = x.shape
    n = wg.shape[1]
    return pl.pallas_call(
        _ffn_up_kernel, grid=(m // tm, n // tn),
        in_specs=[pl.BlockSpec((tm, k), lambda i, j: (i, 0)), pl.BlockSpec((k, tn), lambda i, j: (0, j)),
                  pl.BlockSpec((k, tn), lambda i, j: (0, j))],
        out_specs=pl.BlockSpec((tm, tn), lambda i, j: (i, j)),
        out_shape=jax.ShapeDtypeStruct((m, n), BF16),
        compiler_params=_cparams(2), name="ffn_up")(x, wg, wu)


def _mm_res_ln_kernel(a_ref, b_ref, res_ref, g_ref, beta_ref, o_ref, obf_ref, acc_ref):
    k = pl.program_id(1)

    @pl.when(k == 0)
    def _():
        acc_ref[...] = jnp.zeros_like(acc_ref)

    acc_ref[...] += _dot(a_ref[...], b_ref[...])

    @pl.when(k == pl.num_programs(1) - 1)
    def _():
        y = DN_ALPHA * res_ref[...] + acc_ref[...]
        mu = jnp.mean(y, axis=-1, keepdims=True)
        yc = y - mu
        var = jnp.mean(yc * yc, axis=-1, keepdims=True)
        out = yc * lax.rsqrt(var + LN_EPS) * g_ref[...] + beta_ref[...]
        o_ref[...] = out
        obf_ref[...] = out.astype(BF16)


def _mm_res_ln(a, b, res, g, beta, tm, tk):
    m, k = a.shape
    n = b.shape[1]
    return pl.pallas_call(
        _mm_res_ln_kernel, grid=(m // tm, k // tk),
        in_specs=[pl.BlockSpec((tm, tk), lambda i, j: (i, j)), pl.BlockSpec((tk, n), lambda i, j: (j, 0)),
                  pl.BlockSpec((tm, n), lambda i, j: (i, 0)), pl.BlockSpec((1, n), lambda i, j: (0, 0)),
                  pl.BlockSpec((1, n), lambda i, j: (0, 0))],
        out_specs=[pl.BlockSpec((tm, n), lambda i, j: (i, 0)), pl.BlockSpec((tm, n), lambda i, j: (i, 0))],
        out_shape=[jax.ShapeDtypeStruct((m, n), F32), jax.ShapeDtypeStruct((m, n), BF16)],
        scratch_shapes=[pltpu.VMEM((tm, n), F32)],
        compiler_params=_cparams(2), name="mm_res_ln")(a, b, res, g.reshape(1, n), beta.reshape(1, n))


def _merge_kernel(x_ref, wg_ref, bg_ref, br_ref, wb_ref, o_ref, acc_ref):
    n = pl.program_id(2)
    gate = _sigmoid(_dot(x_ref[...], wg_ref[...]) + bg_ref[...])
    contrib = gate * _dot(br_ref[0], wb_ref[0])

    @pl.when(n == 0)
    def _():
        acc_ref[...] = contrib

    @pl.when(n > 0)
    def _():
        acc_ref[...] += contrib

    @pl.when(n == N_BRANCH - 1)
    def _():
        o_ref[...] = acc_ref[...].astype(o_ref.dtype)


def _merge(x, w_gate, b_gate, branches, w_branch, tm, td):
    m = x.shape[0]
    nd = D_MODEL // td
    return pl.pallas_call(
        _merge_kernel, grid=(m // tm, nd, N_BRANCH),
        in_specs=[pl.BlockSpec((tm, D_MODEL), lambda i, d, n: (i, 0)),
                  pl.BlockSpec((D_MODEL, td), lambda i, d, n: (0, n * nd + d)),
                  pl.BlockSpec((1, td), lambda i, d, n: (0, n * nd + d)),
                  pl.BlockSpec((1, tm, BRANCH_WIDTH), lambda i, d, n: (n, i, 0)),
                  pl.BlockSpec((1, BRANCH_WIDTH, td), lambda i, d, n: (n, 0, d))],
        out_specs=pl.BlockSpec((tm, td), lambda i, d, n: (i, d)),
        out_shape=jax.ShapeDtypeStruct((m, D_MODEL), BF16),
        scratch_shapes=[pltpu.VMEM((tm, td), F32)],
        compiler_params=_cparams(3), name="merge")(x, w_gate, b_gate, branches, w_branch)


def _softmax_step(s, m_ref, l_ref, acc_ref, v_bf):
    m_prev = m_ref[...]
    m_new = jnp.maximum(m_prev, jnp.max(s, axis=-1, keepdims=True))
    alpha = jnp.exp(m_prev - m_new)
    p = jnp.exp(s - m_new)
    l_ref[...] = alpha * l_ref[...] + jnp.sum(p, axis=-1, keepdims=True)
    acc_ref[...] = alpha * acc_ref[...] + _dot(p.astype(BF16), v_bf)
    m_ref[...] = m_new


def _mla_prompt_kernel(ql_ref, qr_ref, c_ref, kr_ref, wuv_ref, o_ref, m_ref, l_ref, acc_ref, *, tq, tk):
    qi = pl.program_id(1)
    kj = pl.program_id(2)
    rows = N_HEADS * tq

    @pl.when(kj == 0)
    def _():
        m_ref[...] = jnp.full_like(m_ref, NEG_INF)
        l_ref[...] = jnp.zeros_like(l_ref)
        acc_ref[...] = jnp.zeros_like(acc_ref)

    @pl.when(kj * tk <= qi * tq + tq - 1)
    def _():
        ql = ql_ref[...].reshape(rows, MLA_KV_RANK)
        qr = qr_ref[...].reshape(rows, MLA_ROPE_DIM)
        c = c_ref[...]
        s = (_dot_nt(ql, c) + _dot_nt(qr, kr_ref[...])) * MLA_SCALE
        qpos = qi * tq + (lax.broadcasted_iota(jnp.int32, (rows, tk), 0) & (tq - 1))
        kpos = kj * tk + lax.broadcasted_iota(jnp.int32, (rows, tk), 1)
        s = jnp.where(kpos <= qpos, s, NEG_INF)
        _softmax_step(s, m_ref, l_ref, acc_ref, c)

    @pl.when(kj == pl.num_programs(2) - 1)
    def _():
        o = (acc_ref[...] / l_ref[...]).astype(BF16)
        for h in range(N_HEADS):
            o_ref[:, h * MLA_V_DIM:(h + 1) * MLA_V_DIM] = _dot(o[h * tq:(h + 1) * tq], wuv_ref[h]).astype(o_ref.dtype)


def _mla_prompt(ql, qr, c, kr, wuv, tq=256, tk=512):
    nq = SEQ // tq
    nk = SEQ // tk

    def kv_map(b, i, j):
        return (b * nk + jnp.minimum(j, (i * tq + tq - 1) // tk), 0)

    return pl.pallas_call(
        functools.partial(_mla_prompt_kernel, tq=tq, tk=tk), grid=(BATCH, nq, nk),
        in_specs=[pl.BlockSpec((N_HEADS, tq, MLA_KV_RANK), lambda b, i, j: (0, b * nq + i, 0)),
                  pl.BlockSpec((N_HEADS, tq, MLA_ROPE_DIM), lambda b, i, j: (0, b * nq + i, 0)),
                  pl.BlockSpec((tk, MLA_KV_RANK), kv_map),
                  pl.BlockSpec((tk, MLA_ROPE_DIM), kv_map),
                  pl.BlockSpec((N_HEADS, MLA_KV_RANK, MLA_V_DIM), lambda b, i, j: (0, 0, 0))],
        out_specs=pl.BlockSpec((tq, BRANCH_WIDTH), lambda b, i, j: (b * nq + i, 0)),
        out_shape=jax.ShapeDtypeStruct((BATCH * SEQ, BRANCH_WIDTH), BF16),
        scratch_shapes=[pltpu.VMEM((N_HEADS * tq, 1), F32), pltpu.VMEM((N_HEADS * tq, 1), F32),
                        pltpu.VMEM((N_HEADS * tq, MLA_KV_RANK), F32)],
        compiler_params=_cparams(3), name="mla_prompt")(ql, qr, c, kr, wuv)


def _split_bf16(x):
    hi = x.astype(BF16)
    lo = (x - hi.astype(F32)).astype(BF16)
    return hi, lo


def _topk_select(gate, n_valid, n_cand):
    lane = lax.broadcasted_iota(jnp.int32, gate.shape, 1)
    gm = jnp.where(lane < n_valid, gate, NEG_INF)
    rank = jnp.zeros(gate.shape, jnp.int32)
    for a in range(n_cand):
        ga = gm[:, a:a + 1]
        beats = (ga > gm) | ((ga == gm) & (lane > a))
        rank = rank + beats.astype(jnp.int32)
    return (rank < MOBA_TOPK) & (lane < n_valid)


def _moba_prompt_kernel(q_ref, k_ref, v_ref, o_ref, m_ref, l_ref, acc_ref, *, tq):
    i = pl.program_id(1)
    rows = N_HEADS * tq
    nb = SEQ // MOBA_BLOCK
    qf = jnp.concatenate([q_ref[:, h * MOBA_HEAD_DIM:(h + 1) * MOBA_HEAD_DIM] for h in range(N_HEADS)], axis=0)
    qb = qf.astype(BF16)

    means = jnp.sum(k_ref[...].reshape(nb, MOBA_BLOCK, MOBA_HEAD_DIM), axis=1) * (1.0 / MOBA_BLOCK)
    means = jnp.concatenate([means, jnp.zeros((128 - nb, MOBA_HEAD_DIM), F32)], axis=0)
    q_hi, q_lo = _split_bf16(qf)
    mn_hi, mn_lo = _split_bf16(means)
    gate = _dot_nt(q_hi, mn_hi) + _dot_nt(q_hi, mn_lo) + _dot_nt(q_lo, mn_hi)
    sel = _topk_select(gate, i, nb).astype(F32)
    lane = lax.broadcasted_iota(jnp.int32, (rows, 128), 1)

    own0 = pl.multiple_of(i * MOBA_BLOCK, MOBA_BLOCK)
    s = _dot_nt(qb, k_ref[pl.ds(own0, MOBA_BLOCK), :].astype(BF16)) * MOBA_SCALE
    qrel = lax.broadcasted_iota(jnp.int32, (rows, MOBA_BLOCK), 0) & (tq - 1)
    krel = lax.broadcasted_iota(jnp.int32, (rows, MOBA_BLOCK), 1)
    s = jnp.where(krel <= qrel, s, NEG_INF)
    m_ref[...] = jnp.full_like(m_ref, NEG_INF)
    l_ref[...] = jnp.zeros_like(l_ref)
    acc_ref[...] = jnp.zeros_like(acc_ref)
    _softmax_step(s, m_ref, l_ref, acc_ref, v_ref[pl.ds(own0, MOBA_BLOCK), :].astype(BF16))

    def past_block(j, carry):
        j0 = pl.multiple_of(j * MOBA_BLOCK, MOBA_BLOCK)
        sj = _dot_nt(qb, k_ref[pl.ds(j0, MOBA_BLOCK), :].astype(BF16)) * MOBA_SCALE
        picked = jnp.max(jnp.where(lane == j, sel, 0.0), axis=-1, keepdims=True) > 0.0
        sj = jnp.where(picked, sj, NEG_INF)
        _softmax_step(sj, m_ref, l_ref, acc_ref, v_ref[pl.ds(j0, MOBA_BLOCK), :].astype(BF16))
        return carry

    lax.fori_loop(0, i, past_block, 0)
    o = acc_ref[...] / l_ref[...]
    for h in range(N_HEADS):
        o_ref[:, h * MOBA_HEAD_DIM:(h + 1) * MOBA_HEAD_DIM] = o[h * tq:(h + 1) * tq].astype(o_ref.dtype)


def _moba_prompt(proj):
    tq = MOBA_BLOCK
    nq = SEQ // tq
    return pl.pallas_call(
        functools.partial(_moba_prompt_kernel, tq=tq), grid=(BATCH, nq),
        in_specs=[pl.BlockSpec((tq, 512), lambda b, i: (b * nq + i, 1536 // 512)),
                  pl.BlockSpec((SEQ, 128), lambda b, i: (b, 2048 // 128)),
                  pl.BlockSpec((SEQ, 128), lambda b, i: (b, 2176 // 128))],
        out_specs=pl.BlockSpec((tq, BRANCH_WIDTH), lambda b, i: (b * nq + i, 0)),
        out_shape=jax.ShapeDtypeStruct((BATCH * SEQ, BRANCH_WIDTH), BF16),
        scratch_shapes=[pltpu.VMEM((N_HEADS * tq, 1), F32), pltpu.VMEM((N_HEADS * tq, 1), F32),
                        pltpu.VMEM((N_HEADS * tq, MOBA_HEAD_DIM), F32)],
        compiler_params=_cparams(2), name="moba_prompt")(proj, proj, proj)


def _ret_log_gamma():
    return np.log1p(-np.exp2(-5.0 - np.arange(RET_HEADS, dtype=np.float32))).astype(np.float32)


def _ret_tables(chunk):
    log_g = jnp.asarray(_ret_log_gamma())
    idx = jnp.arange(chunk, dtype=F32)
    diff = idx[:, None] - idx[None, :]
    decay = jnp.where(diff >= 0, jnp.exp(jnp.maximum(diff, 0.0)[None] * log_g[:, None, None]), 0.0)
    q_dec = jnp.exp((idx + 1.0)[None, :] * log_g[:, None])[..., None]
    k_dec = jnp.exp((chunk - 1.0 - idx)[None, :] * log_g[:, None])[:, None, :]
    c_dec = [float(np.exp(np.float32(chunk) * g)) for g in _ret_log_gamma()]
    return decay, q_dec, k_dec, c_dec


def _ret_prompt_kernel(q_ref, kt_ref, v_ref, dec_ref, qd_ref, kd_ref, o_ref, s_ref, st_ref, *, c_dec):
    c = pl.program_id(1)

    @pl.when(c == 0)
    def _():
        st_ref[...] = jnp.zeros_like(st_ref)

    for h in range(RET_HEADS):
        q = q_ref[h]
        kt = kt_ref[h]
        v = v_ref[h].astype(BF16)
        state = st_ref[h]
        att = _dot(q.astype(BF16), kt.astype(BF16)) * dec_ref[h]
        o = _dot(att.astype(BF16), v) + _dot((q * qd_ref[h]).astype(BF16), state.astype(BF16))
        st_ref[h] = state * c_dec[h] + _dot((kt * kd_ref[h]).astype(BF16), v)
        o_ref[:, h * RET_VAL_DIM:(h + 1) * RET_VAL_DIM] = o

    @pl.when(c == pl.num_programs(1) - 1)
    def _():
        s_ref[0] = st_ref[...]


def _ret_prompt(q, kt, v, chunk=256):
    nc = SEQ // chunk
    decay, q_dec, k_dec, c_dec = _ret_tables(chunk)
    return pl.pallas_call(
        functools.partial(_ret_prompt_kernel, c_dec=c_dec), grid=(BATCH, nc),
        in_specs=[pl.BlockSpec((RET_HEADS, chunk, RET_KEY_DIM), lambda b, c: (0, b * nc + c, 0)),
                  pl.BlockSpec((RET_HEADS, RET_KEY_DIM, chunk), lambda b, c: (0, 0, b * nc + c)),
                  pl.BlockSpec((RET_HEADS, chunk, RET_VAL_DIM), lambda b, c: (0, b * nc + c, 0)),
                  pl.BlockSpec((RET_HEADS, chunk, chunk), lambda b, c: (0, 0, 0)),
                  pl.BlockSpec((RET_HEADS, chunk, 1), lambda b, c: (0, 0, 0)),
                  pl.BlockSpec((RET_HEADS, 1, chunk), lambda b, c: (0, 0, 0))],
        out_specs=[pl.BlockSpec((chunk, BRANCH_WIDTH), lambda b, c: (b * nc + c, 0)),
                   pl.BlockSpec((1, RET_HEADS, RET_KEY_DIM, RET_VAL_DIM), lambda b, c: (b, 0, 0, 0))],
        out_shape=[jax.ShapeDtypeStruct((BATCH * SEQ, BRANCH_WIDTH), F32),
                   jax.ShapeDtypeStruct((BATCH, RET_HEADS, RET_KEY_DIM, RET_VAL_DIM), F32)],
        scratch_shapes=[pltpu.VMEM((RET_HEADS, RET_KEY_DIM, RET_VAL_DIM), F32)],
        compiler_params=_cparams(2), name="ret_prompt")(q, kt, v, decay, q_dec, k_dec)


GLA_C = 64
GLA_SUB = 16


def _gla_chunk(q, k, v, la, state_t):
    c = GLA_C
    row = lax.broadcasted_iota(jnp.int32, (c, c), 0)
    col = lax.broadcasted_iota(jnp.int32, (c, c), 1)
    tril = (col <= row).astype(BF16)
    la_hi = la.astype(BF16)
    r1 = la - la_hi.astype(F32)
    la_mid = r1.astype(BF16)
    la_lo = (r1 - la_mid.astype(F32)).astype(BF16)
    b = _dot(tril, la_hi) + _dot(tril, la_mid) + _dot(tril, la_lo)
    b_last = b[c - 1:c, :]
    v_bf = v.astype(BF16)
    o = _dot_nt((q * jnp.exp(b)).astype(BF16), state_t.astype(BF16))
    k_state = (k * jnp.exp(b_last - b)).astype(BF16)
    new_state_t = state_t * jnp.exp(b_last) + _dot_tn(v_bf, k_state)

    rsub = lax.broadcasted_iota(jnp.int32, (c, GLA_KEY_DIM), 0)
    tri3 = (lax.broadcasted_iota(jnp.int32, (GLA_SUB, GLA_SUB, GLA_KEY_DIM), 1)
            <= lax.broadcasted_iota(jnp.int32, (GLA_SUB, GLA_SUB, GLA_KEY_DIM), 0))
    o_rows = []
    for blk in range(c // GLA_SUB):
        r0 = blk * GLA_SUB
        qi, ki, bi = q[r0:r0 + GLA_SUB], k[r0:r0 + GLA_SUB], b[r0:r0 + GLA_SUB]
        rel = jnp.where(tri3, jnp.exp(jnp.minimum(bi[:, None, :] - bi[None, :, :], 0.0)), 0.0)
        att_d = jnp.sum(qi[:, None, :] * ki[None, :, :] * rel, axis=-1)
        o_blk = o[r0:r0 + GLA_SUB] + _dot(att_d.astype(BF16), v_bf[r0:r0 + GLA_SUB])
        if blk > 0:
            ref_b = b[r0 - 1:r0, :]
            q_off = (qi * jnp.exp(bi - ref_b)).astype(BF16)
            k_off = jnp.where(rsub < r0, k * jnp.exp(jnp.minimum(ref_b - b, 0.0)), 0.0).astype(BF16)
            o_blk = o_blk + _dot(_dot_nt(q_off, k_off).astype(BF16), v_bf)
        o_rows.append(o_blk)
    return jnp.concatenate(o_rows, axis=0), new_state_t


def _gla_prompt_kernel(q_ref, k_ref, v_ref, la_ref, o_ref, s_ref, st_ref, *, n_chunks):
    step = pl.program_id(1)

    @pl.when(step == 0)
    def _():
        st_ref[...] = jnp.zeros_like(st_ref)

    for h in range(GLA_HEADS):
        def chunk_body(ci, carry, h=h):
            r0 = pl.multiple_of(ci * GLA_C, GLA_C)
            o, new_state = _gla_chunk(q_ref[h, pl.ds(r0, GLA_C), :], k_ref[h, pl.ds(r0, GLA_C), :],
                                      v_ref[h, pl.ds(r0, GLA_C), :],
                                      la_ref[h, pl.ds(r0, GLA_C), :], st_ref[h])
            st_ref[h] = new_state
            o_ref[pl.ds(r0, GLA_C), h * GLA_VAL_DIM:(h + 1) * GLA_VAL_DIM] = o
            return carry

        lax.fori_loop(0, n_chunks, chunk_body, 0)

    @pl.when(step == pl.num_programs(1) - 1)
    def _():
        s_ref[0] = st_ref[...]


def _gla_prompt(q, k, v, la, tile=256):
    nt = SEQ // tile
    row_spec = lambda d: pl.BlockSpec((GLA_HEADS, tile, d), lambda b, c: (0, b * nt + c, 0))
    return pl.pallas_call(
        functools.partial(_gla_prompt_kernel, n_chunks=tile // GLA_C), grid=(BATCH, nt),
        in_specs=[row_spec(GLA_KEY_DIM), row_spec(GLA_KEY_DIM), row_spec(GLA_VAL_DIM), row_spec(GLA_KEY_DIM)],
        out_specs=[pl.BlockSpec((tile, BRANCH_WIDTH), lambda b, c: (b * nt + c, 0)),
                   pl.BlockSpec((1, GLA_HEADS, GLA_VAL_DIM, GLA_KEY_DIM), lambda b, c: (b, 0, 0, 0))],
        out_shape=[jax.ShapeDtypeStruct((BATCH * SEQ, BRANCH_WIDTH), F32),
                   jax.ShapeDtypeStruct((BATCH, GLA_HEADS, GLA_VAL_DIM, GLA_KEY_DIM), F32)],
        scratch_shapes=[pltpu.VMEM((GLA_HEADS, GLA_VAL_DIM, GLA_KEY_DIM), F32)],
        compiler_params=_cparams(2), name="gla_prompt")(q, k, v, la)


SAMPLE_NB = 8


def _ret_sample_kernel(q_ref, kt_ref, v_ref, s0_ref, dec_ref, qd_ref, kd_ref, cd_ref, o_ref, s_ref):
    nb = SAMPLE_NB * RET_HEADS
    q = q_ref[...].reshape(nb, DEC_SEQ, RET_KEY_DIM)
    kt = kt_ref[...].reshape(nb, RET_KEY_DIM, DEC_SEQ)
    v = v_ref[...].reshape(nb, DEC_SEQ, RET_VAL_DIM).astype(BF16)
    s0 = s0_ref[...].reshape(nb, RET_KEY_DIM, RET_VAL_DIM)
    att = jnp.einsum('bqd,bdk->bqk', q.astype(BF16), kt.astype(BF16), preferred_element_type=F32) * dec_ref[...]
    o = (jnp.einsum('bqk,bkv->bqv', att.astype(BF16), v, preferred_element_type=F32)
         + jnp.einsum('bqd,bdv->bqv', (q * qd_ref[...]).astype(BF16), s0.astype(BF16), preferred_element_type=F32))
    s_new = s0 * cd_ref[...] + jnp.einsum('bdt,btv->bdv', (kt * kd_ref[...]).astype(BF16), v, preferred_element_type=F32)
    o_ref[...] = o.reshape(SAMPLE_NB, RET_HEADS, DEC_SEQ, RET_VAL_DIM)
    s_ref[...] = s_new.reshape(SAMPLE_NB, RET_HEADS, RET_KEY_DIM, RET_VAL_DIM)


def _ret_sample(q, kt, v, s0):
    decay, q_dec, k_dec, c_dec = _ret_tables(DEC_SEQ)
    rep = lambda t: jnp.tile(t, (SAMPLE_NB,) + (1,) * (t.ndim - 1))
    cd = rep(jnp.asarray(c_dec, F32).reshape(RET_HEADS, 1, 1))
    blk = lambda *dims: pl.BlockSpec((SAMPLE_NB, RET_HEADS) + dims, lambda i: (i, 0, 0, 0))
    full = lambda t: pl.BlockSpec(t.shape, lambda i: (0, 0, 0))
    tabs = [rep(decay), rep(q_dec), rep(k_dec), cd]
    return pl.pallas_call(
        _ret_sample_kernel, grid=(DEC_BATCH // SAMPLE_NB,),
        in_specs=[blk(DEC_SEQ, RET_KEY_DIM), blk(RET_KEY_DIM, DEC_SEQ), blk(DEC_SEQ, RET_VAL_DIM),
                  blk(RET_KEY_DIM, RET_VAL_DIM)] + [full(t) for t in tabs],
        out_specs=[blk(DEC_SEQ, RET_VAL_DIM), blk(RET_KEY_DIM, RET_VAL_DIM)],
        out_shape=[jax.ShapeDtypeStruct((DEC_BATCH, RET_HEADS, DEC_SEQ, RET_VAL_DIM), F32),
                   jax.ShapeDtypeStruct((DEC_BATCH, RET_HEADS, RET_KEY_DIM, RET_VAL_DIM), F32)],
        compiler_params=_cparams(1), name="ret_sample")(q, kt, v, s0, *tabs)


def _gla_sample_kernel(q_ref, k_ref, kt_ref, v_ref, la_ref, lat_ref, s0_ref, o_ref, s_ref):
    nb = SAMPLE_NB * GLA_HEADS
    t_n = DEC_SEQ
    q = q_ref[...].reshape(nb, t_n, GLA_KEY_DIM)
    k = k_ref[...].reshape(nb, t_n, GLA_KEY_DIM)
    v = v_ref[...].reshape(nb, t_n, GLA_VAL_DIM)
    la = la_ref[...].reshape(nb, t_n, GLA_KEY_DIM)
    kt = kt_ref[...].reshape(nb, GLA_KEY_DIM, t_n)
    lat = lat_ref[...].reshape(nb, GLA_KEY_DIM, t_n)
    s0 = s0_ref[...].reshape(nb, GLA_KEY_DIM, GLA_VAL_DIM)
    b_rows, bt_cols = [], []
    run, run_t = None, None
    for t in range(t_n):
        run = la[:, t, :] if run is None else run + la[:, t, :]
        run_t = lat[:, :, t:t + 1] if run_t is None else run_t + lat[:, :, t:t + 1]
        b_rows.append(run)
        bt_cols.append(run_t)
    b_last_t = bt_cols[-1]
    q_state = jnp.stack([q[:, t, :] * jnp.exp(b_rows[t]) for t in range(t_n)], axis=1)
    o = jnp.einsum('bqd,bdv->bqv', q_state.astype(BF16), s0.astype(BF16), preferred_element_type=F32)
    o_rows = []
    for i in range(t_n):
        o_i = o[:, i, :]
        for j in range(i + 1):
            w = jnp.sum(q[:, i, :] * k[:, j, :] * jnp.exp(jnp.minimum(b_rows[i] - b_rows[j], 0.0)),
                        axis=-1, keepdims=True)
            o_i = o_i + w * v[:, j, :]
        o_rows.append(o_i)
    k_state_t = jnp.concatenate([kt[:, :, t:t + 1] * jnp.exp(b_last_t - bt_cols[t]) for t in range(t_n)], axis=2)
    s_new = s0 * jnp.exp(b_last_t) + jnp.einsum('bdt,btv->bdv', k_state_t.astype(BF16), v.astype(BF16),
                                                  preferred_element_type=F32)
    o_ref[...] = jnp.stack(o_rows, axis=1).reshape(SAMPLE_NB, GLA_HEADS, t_n, GLA_VAL_DIM)
    s_ref[...] = s_new.reshape(SAMPLE_NB, GLA_HEADS, GLA_KEY_DIM, GLA_VAL_DIM)


def _gla_sample(q, k, v, la, s0):
    kt = jnp.swapaxes(k, 2, 3)
    lat = jnp.swapaxes(la, 2, 3)
    blk = lambda *dims: pl.BlockSpec((SAMPLE_NB, GLA_HEADS) + dims, lambda i: (i, 0, 0, 0))
    return pl.pallas_call(
        _gla_sample_kernel, grid=(DEC_BATCH // SAMPLE_NB,),
        in_specs=[blk(DEC_SEQ, GLA_KEY_DIM), blk(DEC_SEQ, GLA_KEY_DIM), blk(GLA_KEY_DIM, DEC_SEQ),
                  blk(DEC_SEQ, GLA_VAL_DIM), blk(DEC_SEQ, GLA_KEY_DIM), blk(GLA_KEY_DIM, DEC_SEQ),
                  blk(GLA_KEY_DIM, GLA_VAL_DIM)],
        out_specs=[blk(DEC_SEQ, GLA_VAL_DIM), blk(GLA_KEY_DIM, GLA_VAL_DIM)],
        out_shape=[jax.ShapeDtypeStruct((DEC_BATCH, GLA_HEADS, DEC_SEQ, GLA_VAL_DIM), F32),
                   jax.ShapeDtypeStruct((DEC_BATCH, GLA_HEADS, GLA_KEY_DIM, GLA_VAL_DIM), F32)],
        compiler_params=_cparams(1), name="gla_sample")(q, k, kt, v, la, lat, s0)


PAGES_PER_STEP = 32
KEYS_PER_STEP = PAGES_PER_STEP * PAGE_SIZE
N_CHUNKS = N_PAGES // PAGES_PER_STEP
Q_ROWS = N_HEADS * DEC_SEQ


def _page_copies(pt_ref, layer, seq, chunk, slot, streams):
    copies = []
    for i in range(PAGES_PER_STEP):
        page = pt_ref[seq, chunk * PAGES_PER_STEP + i]
        for hbm, buf, sem in streams:
            copies.append(pltpu.make_async_copy(hbm.at[layer, page], buf.at[slot, pl.ds(i * PAGE_SIZE, PAGE_SIZE)],
                                                sem.at[slot]))
    return copies


def _new_token_mask():
    tok = lax.broadcasted_iota(jnp.int32, (Q_ROWS, DEC_SEQ), 0) & (DEC_SEQ - 1)
    key = lax.broadcasted_iota(jnp.int32, (Q_ROWS, DEC_SEQ), 1)
    return key <= tok


def _mla_sample_kernel(pt_ref, ql_ref, qr_ref, cn_ref, kn_ref, lat_hbm, rope_hbm, o_ref,
                       lat_buf, rope_buf, lat_sem, rope_sem, m_ref, l_ref, acc_ref, *, layer):
    s = pl.program_id(0)
    c = pl.program_id(1)
    g = s * N_CHUNKS + c
    slot = lax.rem(g, 2)
    streams = ((lat_hbm, lat_buf, lat_sem), (rope_hbm, rope_buf, rope_sem))

    @pl.when(g == 0)
    def _():
        for cp in _page_copies(pt_ref, layer, 0, 0, 0, streams):
            cp.start()

    @pl.when(g + 1 < DEC_BATCH * N_CHUNKS)
    def _():
        nxt = g + 1
        for cp in _page_copies(pt_ref, layer, nxt // N_CHUNKS, lax.rem(nxt, N_CHUNKS), 1 - slot, streams):
            cp.start()

    for cp in _page_copies(pt_ref, layer, s, c, slot, streams):
        cp.wait()

    @pl.when(c == 0)
    def _():
        m_ref[...] = jnp.full_like(m_ref, NEG_INF)
        l_ref[...] = jnp.zeros_like(l_ref)
        acc_ref[...] = jnp.zeros_like(acc_ref)

    ql = ql_ref[0].astype(BF16)
    qr = qr_ref[0].astype(BF16)
    lat = lat_buf[slot].astype(BF16)
    sc = (_dot_nt(ql, lat) + _dot_nt(qr, rope_buf[slot].astype(BF16))) * MLA_SCALE
    _softmax_step(sc, m_ref, l_ref, acc_ref, lat)

    @pl.when(c == N_CHUNKS - 1)
    def _():
        cn = cn_ref[0].astype(BF16)
        sn = (_dot_nt(ql, cn) + _dot_nt(qr, kn_ref[0].astype(BF16))) * MLA_SCALE
        sn = jnp.where(_new_token_mask(), sn, NEG_INF)
        _softmax_step(sn, m_ref, l_ref, acc_ref, cn)
        o_ref[0] = acc_ref[...] / l_ref[...]


def _mla_sample(page_table, ql, qr, cn, kn, cache_lat, cache_rope, layer):
    seq_blk = lambda r, d: pl.BlockSpec((1, r, d), lambda s, c, pt: (s, 0, 0))
    grid_spec = pltpu.PrefetchScalarGridSpec(
        num_scalar_prefetch=1, grid=(DEC_BATCH, N_CHUNKS),
        in_specs=[seq_blk(Q_ROWS, MLA_KV_RANK), seq_blk(Q_ROWS, MLA_ROPE_DIM), seq_blk(DEC_SEQ, MLA_KV_RANK),
                  seq_blk(DEC_SEQ, MLA_ROPE_DIM), pl.BlockSpec(memory_space=pl.ANY), pl.BlockSpec(memory_space=pl.ANY)],
        out_specs=seq_blk(Q_ROWS, MLA_KV_RANK),
        scratch_shapes=[pltpu.VMEM((2, KEYS_PER_STEP, MLA_KV_RANK), F32), pltpu.VMEM((2, KEYS_PER_STEP, MLA_ROPE_DIM), F32),
                        pltpu.SemaphoreType.DMA((2,)), pltpu.SemaphoreType.DMA((2,)),
                        pltpu.VMEM((Q_ROWS, 1), F32), pltpu.VMEM((Q_ROWS, 1), F32), pltpu.VMEM((Q_ROWS, MLA_KV_RANK), F32)])
    return pl.pallas_call(
        functools.partial(_mla_sample_kernel, layer=layer), grid_spec=grid_spec,
        out_shape=jax.ShapeDtypeStruct((DEC_BATCH, Q_ROWS, MLA_KV_RANK), F32),
        compiler_params=_cparams(2), name="mla_sample")(page_table, ql, qr, cn, kn, cache_lat, cache_rope)


N_MOBA_BLOCKS = PAST_LEN // MOBA_BLOCK
BLOCKS_PER_STEP = KEYS_PER_STEP // MOBA_BLOCK


def _moba_sample_kernel(pt_ref, q_ref, kn_ref, vn_ref, k_hbm, v_hbm, o_ref,
                        kv_buf, kv_sem, s_ref, p_ref, gate_ref, l_ref, acc_ref, *, layer):
    s = pl.program_id(0)
    c = pl.program_id(1)
    steps = 2 * N_CHUNKS
    g = s * steps + c
    slot = lax.rem(g, 2)

    def copies(hbm, seq, step, slot_):
        return _page_copies(pt_ref, layer, seq, lax.rem(step, N_CHUNKS), slot_, ((hbm, kv_buf, kv_sem),))

    def start(seq, step, slot_):
        @pl.when(step < N_CHUNKS)
        def _():
            for cp in copies(k_hbm, seq, step, slot_):
                cp.start()

        @pl.when(step >= N_CHUNKS)
        def _():
            for cp in copies(v_hbm, seq, step, slot_):
                cp.start()

    @pl.when(g == 0)
    def _():
        start(0, 0, 0)

    @pl.when(g + 1 < DEC_BATCH * steps)
    def _():
        nxt = g + 1
        start(nxt // steps, lax.rem(nxt, steps), 1 - slot)

    for cp in copies(k_hbm, s, c, slot):
        cp.wait()

    qb = q_ref[0].astype(BF16)
    lane = lax.broadcasted_iota(jnp.int32, (Q_ROWS, 128), 1)

    @pl.when(c == 0)
    def _():
        gate_ref[...] = jnp.zeros_like(gate_ref)

    @pl.when(c < N_CHUNKS)
    def _():
        raw = _dot_nt(qb, kv_buf[slot].astype(BF16))
        s_ref[c] = raw
        gate = gate_ref[...]
        for blk in range(BLOCKS_PER_STEP):
            tot = jnp.sum(raw[:, blk * MOBA_BLOCK:(blk + 1) * MOBA_BLOCK], axis=-1, keepdims=True)
            gate = jnp.where(lane == c * BLOCKS_PER_STEP + blk, tot * (1.0 / MOBA_BLOCK), gate)
        gate_ref[...] = gate

    @pl.when(c == N_CHUNKS - 1)
    def _():
        sel = _topk_select(gate_ref[...], N_MOBA_BLOCKS, N_MOBA_BLOCKS).astype(BF16)
        kn = kn_ref[0].astype(BF16)
        sn = jnp.where(_new_token_mask(), _dot_nt(qb, kn) * MOBA_SCALE, NEG_INF)
        m = jnp.max(sn, axis=-1, keepdims=True)
        key_blk = lax.broadcasted_iota(jnp.int32, (128, KEYS_PER_STEP), 1) // MOBA_BLOCK
        blk_row = lax.broadcasted_iota(jnp.int32, (128, KEYS_PER_STEP), 0)
        masked = []
        for ci in range(N_CHUNKS):
            expand = (blk_row == key_blk + ci * BLOCKS_PER_STEP).astype(BF16)
            picked = _dot(sel, expand) > 0.5
            sc = jnp.where(picked, s_ref[ci] * MOBA_SCALE, NEG_INF)
            masked.append(sc)
            m = jnp.maximum(m, jnp.max(sc, axis=-1, keepdims=True))
        pn = jnp.exp(sn - m)
        l = jnp.sum(pn, axis=-1, keepdims=True)
        for ci in range(N_CHUNKS):
            p = jnp.exp(masked[ci] - m)
            l = l + jnp.sum(p, axis=-1, keepdims=True)
            p_ref[ci] = p.astype(BF16)
        l_ref[...] = l
        acc_ref[...] = _dot(pn.astype(BF16), vn_ref[0].astype(BF16))

    @pl.when(c >= N_CHUNKS)
    def _():
        acc_ref[...] += _dot(p_ref[c - N_CHUNKS], kv_buf[slot].astype(BF16))

    @pl.when(c == steps - 1)
    def _():
        o_ref[0] = acc_ref[...] / l_ref[...]


def _moba_sample(page_table, q, kn, vn, cache_k, cache_v, layer):
    seq_blk = lambda r, d: pl.BlockSpec((1, r, d), lambda s, c, pt: (s, 0, 0))
    grid_spec = pltpu.PrefetchScalarGridSpec(
        num_scalar_prefetch=1, grid=(DEC_BATCH, 2 * N_CHUNKS),
        in_specs=[seq_blk(Q_ROWS, MOBA_HEAD_DIM), seq_blk(DEC_SEQ, MOBA_HEAD_DIM), seq_blk(DEC_SEQ, MOBA_HEAD_DIM),
                  pl.BlockSpec(memory_space=pl.ANY), pl.BlockSpec(memory_space=pl.ANY)],
        out_specs=seq_blk(Q_ROWS, MOBA_HEAD_DIM),
        scratch_shapes=[pltpu.VMEM((2, KEYS_PER_STEP, MOBA_HEAD_DIM), F32), pltpu.SemaphoreType.DMA((2,)),
                        pltpu.VMEM((N_CHUNKS, Q_ROWS, KEYS_PER_STEP), F32),
                        pltpu.VMEM((N_CHUNKS, Q_ROWS, KEYS_PER_STEP), BF16),
                        pltpu.VMEM((Q_ROWS, 128), F32), pltpu.VMEM((Q_ROWS, 1), F32),
                        pltpu.VMEM((Q_ROWS, MOBA_HEAD_DIM), F32)])
    return pl.pallas_call(
        functools.partial(_moba_sample_kernel, layer=layer), grid_spec=grid_spec,
        out_shape=jax.ShapeDtypeStruct((DEC_BATCH, Q_ROWS, MOBA_HEAD_DIM), F32),
        compiler_params=_cparams(2), name="moba_sample")(page_table, q, kn, vn, cache_k, cache_v)


def _pack_w_main(w):
    pad = jnp.zeros((D_MODEL, 48), w.dtype)
    return jnp.concatenate([w[:, :2944], w[:, 2944:3008], w[:, 4032:4048], pad, w[:, 3008:4032], w[:, 4048:4560]],
                           axis=1).astype(BF16)


def _rms(x, g):
    return x * lax.rsqrt(jnp.mean(x * x, -1, keepdims=True) + RMS_EPS) * g


def _rope(x, pos, inv_freq):
    ang = pos.astype(F32)[:, None] * inv_freq[None, :]
    shape = (x.shape[0],) + (1,) * (x.ndim - 2) + (inv_freq.shape[0],)
    cos = jnp.cos(ang).reshape(shape)
    sin = jnp.sin(ang).reshape(shape)
    x1, x2 = jnp.split(x, 2, axis=-1)
    return jnp.concatenate([x1 * cos - x2 * sin, x2 * cos + x1 * sin], axis=-1)


def _heads_major(t):
    return jnp.swapaxes(t, 0, 1)


def _layer_weights(l, W):
    w_uq = W['mla_w_uq'][l]
    return dict(
        w_main=_pack_w_main(W['w_in'][l]),
        w_gate=W['w_in'][l][:, GATE_COL0:].astype(BF16),
        b_gate=W['b_gate'][l].reshape(1, N_BRANCH * D_MODEL),
        w_uq=jnp.concatenate([w_uq[:, :, :MLA_NOPE_DIM].reshape(MLA_Q_RANK, -1),
                              w_uq[:, :, MLA_NOPE_DIM:].reshape(MLA_Q_RANK, -1)], axis=1).astype(BF16),
        w_uk_t=jnp.transpose(W['mla_w_uk'][l], (1, 2, 0)).astype(BF16),
        w_uv=jnp.transpose(W['mla_w_uv'][l], (1, 0, 2)).astype(BF16),
        w_branch=W['w_branch'][l].astype(BF16),
        w_out=W['w_out'][l].astype(BF16),
        ffn_w_gate=W['ffn_w_gate'][l].astype(BF16),
        ffn_w_up=W['ffn_w_up'][l].astype(BF16),
        ffn_w_down=W['ffn_w_down'][l].astype(BF16),
    )


def _project(x_bf, pos, l, W, LW, tm):
    m = x_bf.shape[0]
    proj = _matmul(x_bf, LW['w_main'], tm, 512, F32)
    ret_inv = 1.0 / (ROPE_THETA ** jnp.linspace(0.0, 1.0, RET_KEY_DIM // 2, dtype=F32))
    mla_inv = 1.0 / (ROPE_THETA ** (jnp.arange(0, MLA_ROPE_DIM, 2, dtype=F32) / MLA_ROPE_DIM))
    p = {'proj': proj}
    p['ret_q'] = _rope(proj[:, 0:256].reshape(m, RET_HEADS, RET_KEY_DIM), pos, ret_inv)
    p['ret_k'] = _rope(proj[:, 256:512].reshape(m, RET_HEADS, RET_KEY_DIM), pos, ret_inv) * (RET_KEY_DIM ** -0.5)
    p['ret_v'] = proj[:, 512:1024].reshape(m, RET_HEADS, RET_VAL_DIM)
    p['ret_g'] = proj[:, 1024:1536]
    p['moba_q'] = proj[:, 1536:2048]
    p['moba_k'] = proj[:, 2048:2176]
    p['moba_v'] = proj[:, 2176:2304]
    cq = _rms(proj[:, 2304:2688], W['mla_q_norm_g'][l])
    qm = _matmul(cq.astype(BF16), LW['w_uq'], tm, 768, F32)
    q_nope = _heads_major(qm[:, :512].reshape(m, MLA_HEADS, MLA_NOPE_DIM)).astype(BF16)
    p['mla_q_lat'] = _matmul_heads(q_nope, LW['w_uk_t'], tm, F32)
    p['mla_q_rope'] = _heads_major(_rope(qm[:, 512:].reshape(m, MLA_HEADS, MLA_ROPE_DIM), pos, mla_inv))
    p['mla_c'] = _rms(proj[:, 2688:2944], W['mla_kv_norm_g'][l])
    p['mla_kr'] = _rope(proj[:, 2944:3008], pos, mla_inv)
    ga = proj[:, 3008:3024]
    p['gla_q'] = proj[:, 3072:3328].reshape(m, GLA_HEADS, GLA_KEY_DIM) * (GLA_KEY_DIM ** -0.5)
    p['gla_k'] = proj[:, 3328:3584].reshape(m, GLA_HEADS, GLA_KEY_DIM)
    p['gla_v'] = proj[:, 3584:4096].reshape(m, GLA_HEADS, GLA_VAL_DIM)
    a_logit = _matmul(ga.astype(BF16), W['gla_w_a'][l].reshape(GLA_GATE_RANK, -1).astype(BF16), tm, 256, F32)
    a_logit = a_logit.reshape(m, GLA_HEADS, GLA_KEY_DIM) + W['gla_b_a'][l]
    p['gla_log_a'] = jax.nn.log_sigmoid(a_logit) / GLA_GATE_TAU
    p['gla_r'] = proj[:, 4096:4608]
    return p


def _finish_layer(x, x_bf, p, l, W, LW, ret_o, moba_o, mla_o, gla_o, tm):
    m = x.shape[0]
    ret = (_rms(ret_o.reshape(m, RET_HEADS, RET_VAL_DIM), W['ret_norm_g'][l]).reshape(m, BRANCH_WIDTH)
           * jax.nn.silu(p['ret_g']))
    gla = (_rms(gla_o.reshape(m, GLA_HEADS, GLA_VAL_DIM), W['gla_norm_g'][l]).reshape(m, BRANCH_WIDTH)
           * jax.nn.silu(p['gla_r']))
    branches = jnp.stack([ret.astype(BF16), moba_o.astype(BF16), mla_o.astype(BF16), gla.astype(BF16)], axis=0)
    merged = _merge(x_bf, LW['w_gate'], LW['b_gate'], branches, LW['w_branch'], tm, 512)
    x1, x1_bf = _mm_res_ln(merged, LW['w_out'], x, W['ln1_g'][l], W['ln1_b'][l], 512, 512)
    hidden = _ffn_up(x1_bf, LW['ffn_w_gate'], LW['ffn_w_up'], tm, 512)
    return _mm_res_ln(hidden, LW['ffn_w_down'], x1, W['ln2_g'][l], W['ln2_b'][l], 512, 512)


def _prompt_layer(x, x_bf, l, W, LW):
    m = BATCH * SEQ
    pos = jnp.tile(jnp.arange(SEQ, dtype=jnp.int32), BATCH)
    p = _project(x_bf, pos, l, W, LW, 1024)
    ret_o, ret_s = _ret_prompt(_heads_major(p['ret_q']), jnp.transpose(p['ret_k'], (1, 2, 0)), _heads_major(p['ret_v']))
    gla_o, gla_st = _gla_prompt(_heads_major(p['gla_q']), _heads_major(p['gla_k']), _heads_major(p['gla_v']),
                                _heads_major(p['gla_log_a']))
    moba_o = _moba_prompt(p['proj'])
    mla_o = _mla_prompt(p['mla_q_lat'].astype(BF16), p['mla_q_rope'].astype(BF16), p['mla_c'].astype(BF16),
                        p['mla_kr'].astype(BF16), LW['w_uv'])
    x, x_bf = _finish_layer(x, x_bf, p, l, W, LW, ret_o, moba_o, mla_o, gla_o, 1024)
    new = (p['moba_k'].reshape(BATCH, SEQ, 1, MOBA_HEAD_DIM), p['moba_v'].reshape(BATCH, SEQ, 1, MOBA_HEAD_DIM),
           p['mla_c'].reshape(BATCH, SEQ, MLA_KV_RANK), p['mla_kr'].reshape(BATCH, SEQ, MLA_ROPE_DIM),
           ret_s, jnp.swapaxes(gla_st, 2, 3))
    return x, x_bf, new


def _sample_layer(x, x_bf, l, W, LW, caches, page_table):
    m = DEC_BATCH * DEC_SEQ
    pos = jnp.tile(PAST_LEN + jnp.arange(DEC_SEQ, dtype=jnp.int32), DEC_BATCH)
    p = _project(x_bf, pos, l, W, LW, 512)
    seq_heads = lambda t, d: jnp.swapaxes(t.reshape(DEC_BATCH, DEC_SEQ, N_HEADS, d), 1, 2)
    back = lambda t, d: jnp.swapaxes(t, 1, 2).reshape(m, N_HEADS * d)

    ret_k = seq_heads(p['ret_k'], RET_KEY_DIM)
    ret_o, ret_s = _ret_sample(seq_heads(p['ret_q'], RET_KEY_DIM), jnp.swapaxes(ret_k, 2, 3),
                               seq_heads(p['ret_v'], RET_VAL_DIM), caches['state_retention'][l])
    gla_o, gla_s = _gla_sample(seq_heads(p['gla_q'], GLA_KEY_DIM), seq_heads(p['gla_k'], GLA_KEY_DIM),
                               seq_heads(p['gla_v'], GLA_VAL_DIM), seq_heads(p['gla_log_a'], GLA_KEY_DIM),
                               caches['state_gla'][l])
    moba_q = seq_heads(p['moba_q'], MOBA_HEAD_DIM).reshape(DEC_BATCH, Q_ROWS, MOBA_HEAD_DIM)
    moba_o = _moba_sample(page_table, moba_q, p['moba_k'].reshape(DEC_BATCH, DEC_SEQ, MOBA_HEAD_DIM),
                          p['moba_v'].reshape(DEC_BATCH, DEC_SEQ, MOBA_HEAD_DIM),
                          caches['cache_moba_k'].reshape(DEPTH, -1, PAGE_SIZE, MOBA_HEAD_DIM),
                          caches['cache_moba_v'].reshape(DEPTH, -1, PAGE_SIZE, MOBA_HEAD_DIM), l)
    moba_o = back(moba_o.reshape(DEC_BATCH, N_HEADS, DEC_SEQ, MOBA_HEAD_DIM), MOBA_HEAD_DIM)

    to_rows = lambda t, d: jnp.swapaxes(t.reshape(N_HEADS, DEC_BATCH, DEC_SEQ, d), 0, 1).reshape(DEC_BATCH, Q_ROWS, d)
    mla_lat = _mla_sample(page_table, to_rows(p['mla_q_lat'], MLA_KV_RANK), to_rows(p['mla_q_rope'], MLA_ROPE_DIM),
                          p['mla_c'].reshape(DEC_BATCH, DEC_SEQ, MLA_KV_RANK),
                          p['mla_kr'].reshape(DEC_BATCH, DEC_SEQ, MLA_ROPE_DIM),
                          caches['cache_mla_latent'], caches['cache_mla_rope'], l)
    mla_lat = jnp.swapaxes(mla_lat.reshape(DEC_BATCH, N_HEADS, DEC_SEQ, MLA_KV_RANK), 0, 1).reshape(N_HEADS, m, MLA_KV_RANK)
    mla_o = _matmul_heads(mla_lat.astype(BF16), LW['w_uv'], 512, BF16)
    mla_o = jnp.swapaxes(mla_o, 0, 1).reshape(m, BRANCH_WIDTH)

    x, x_bf = _finish_layer(x, x_bf, p, l, W, LW, back(ret_o, RET_VAL_DIM), moba_o, mla_o, back(gla_o, GLA_VAL_DIM), 512)
    new = (p['moba_k'].reshape(DEC_BATCH, DEC_SEQ, 1, MOBA_HEAD_DIM), p['moba_v'].reshape(DEC_BATCH, DEC_SEQ, 1, MOBA_HEAD_DIM),
           p['mla_c'].reshape(DEC_BATCH, DEC_SEQ, MLA_KV_RANK), p['mla_kr'].reshape(DEC_BATCH, DEC_SEQ, MLA_ROPE_DIM),
           ret_s, gla_s)
    return x, x_bf, new


def kernel(x_prompt, x_sample, cache_moba_k, cache_moba_v, cache_mla_latent, cache_mla_rope, state_retention, state_gla, page_table, w_in, b_gate, ret_norm_g, mla_q_norm_g, mla_w_uq, mla_kv_norm_g, mla_w_uk, mla_w_uv, gla_w_a, gla_b_a, gla_norm_g, w_branch, w_out, ln1_g, ln1_b, ffn_w_gate, ffn_w_up, ffn_w_down, ln2_g, ln2_b):
    W = dict(w_in=w_in, b_gate=b_gate, ret_norm_g=ret_norm_g, mla_q_norm_g=mla_q_norm_g, mla_w_uq=mla_w_uq,
             mla_kv_norm_g=mla_kv_norm_g, mla_w_uk=mla_w_uk, mla_w_uv=mla_w_uv, gla_w_a=gla_w_a, gla_b_a=gla_b_a,
             gla_norm_g=gla_norm_g, w_branch=w_branch, w_out=w_out, ln1_g=ln1_g, ln1_b=ln1_b,
             ffn_w_gate=ffn_w_gate, ffn_w_up=ffn_w_up, ffn_w_down=ffn_w_down, ln2_g=ln2_g, ln2_b=ln2_b)
    caches = dict(cache_moba_k=cache_moba_k, cache_moba_v=cache_moba_v, cache_mla_latent=cache_mla_latent,
                  cache_mla_rope=cache_mla_rope, state_retention=state_retention, state_gla=state_gla)
    xp = x_prompt.reshape(BATCH * SEQ, D_MODEL)
    xs = x_sample.reshape(DEC_BATCH * DEC_SEQ, D_MODEL)
    xp_bf, xs_bf = xp.astype(BF16), xs.astype(BF16)
    new_p, new_s = [], []
    for l in range(DEPTH):
        LW = _layer_weights(l, W)
        xp, xp_bf, st_p = _prompt_layer(xp, xp_bf, l, W, LW)
        xs, xs_bf, st_s = _sample_layer(xs, xs_bf, l, W, LW, caches, page_table)
        new_p.append(st_p)
        new_s.append(st_s)
    stack = lambda states, i: jnp.stack([s[i] for s in states], axis=0)
    return ((xp.reshape(BATCH, SEQ, D_MODEL), xs.reshape(DEC_BATCH, DEC_SEQ, D_MODEL))
            + tuple(stack(new_p, i) for i in range(6)) + tuple(stack(new_s, i) for i in range(6)))
```

```python
import functools

import numpy as np
import jax
import jax.numpy as jnp
from jax import lax
from jax.experimental import pallas as pl
from jax.experimental.pallas import tpu as pltpu

D_MODEL = 2048
BATCH = 4
SEQ = 2048
DEPTH = 2
DEC_BATCH = 128
DEC_SEQ = 4
PAST_LEN = 16384
PAGE_SIZE = 128
N_PAGES = PAST_LEN // PAGE_SIZE

RET_HEADS = 4
RET_KEY_DIM = 64
RET_VAL_DIM = 128
MOBA_HEADS = 4
MOBA_HEAD_DIM = 128
MOBA_BLOCK = 256
MOBA_TOPK = 3
MLA_HEADS = 4
MLA_Q_RANK = 384
MLA_KV_RANK = 256
MLA_NOPE_DIM = 128
MLA_ROPE_DIM = 64
MLA_V_DIM = 128
GLA_HEADS = 4
GLA_KEY_DIM = 64
GLA_VAL_DIM = 128
GLA_GATE_RANK = 16
GLA_GATE_TAU = 16.0
N_BRANCH = 4
BRANCH_WIDTH = 512
FFN_HIDDEN = 5632
ROPE_THETA = 10000.0
DN_ALPHA = (2 * DEPTH) ** 0.25
LN_EPS = 1e-5
RMS_EPS = 1e-6
F32 = jnp.float32
BF16 = jnp.bfloat16

N_HEADS = 4
MAIN_COLS = 4608
GATE_COL0 = 4560
MLA_SCALE = (MLA_NOPE_DIM + MLA_ROPE_DIM) ** -0.5
MOBA_SCALE = MOBA_HEAD_DIM ** -0.5

VMEM_LIMIT = 56 * 1024 * 1024
NEG_INF = float("-inf")


def _cparams(n_axes):
    return pltpu.CompilerParams(dimension_semantics=("arbitrary",) * n_axes, vmem_limit_bytes=VMEM_LIMIT)


def _dot(a, b):
    return jnp.dot(a, b, preferred_element_type=F32)


def _dot_nt(a, b):
    return lax.dot_general(a, b, (((1,), (1,)), ((), ())), preferred_element_type=F32)


def _dot_tn(a, b):
    return lax.dot_general(a, b, (((0,), (0,)), ((), ())), preferred_element_type=F32)


def _sigmoid(x):
    return 1.0 / (1.0 + jnp.exp(-x))


def _mm_kernel(a_ref, b_ref, o_ref):
    o_ref[...] = _dot(a_ref[...], b_ref[...]).astype(o_ref.dtype)


def _matmul(a, b, tm, tn, out_dtype):
    m, k = a.shape
    n = b.shape[1]
    return pl.pallas_call(
        _mm_kernel, grid=(m // tm, n // tn),
        in_specs=[pl.BlockSpec((tm, k), lambda i, j: (i, 0)), pl.BlockSpec((k, tn), lambda i, j: (0, j))],
        out_specs=pl.BlockSpec((tm, tn), lambda i, j: (i, j)),
        out_shape=jax.ShapeDtypeStruct((m, n), out_dtype),
        compiler_params=_cparams(2), name="mm")(a, b)


def _mm_heads_kernel(a_ref, b_ref, o_ref):
    o_ref[0] = _dot(a_ref[0], b_ref[0]).astype(o_ref.dtype)


def _matmul_heads(a, b, tm, out_dtype):
    h, m, k = a.shape
    n = b.shape[2]
    return pl.pallas_call(
        _mm_heads_kernel, grid=(h, m // tm),
        in_specs=[pl.BlockSpec((1, tm, k), lambda g, i: (g, i, 0)), pl.BlockSpec((1, k, n), lambda g, i: (g, 0, 0))],
        out_specs=pl.BlockSpec((1, tm, n), lambda g, i: (g, i, 0)),
        out_shape=jax.ShapeDtypeStruct((h, m, n), out_dtype),
        compiler_params=_cparams(2), name="mm_heads")(a, b)


def _ffn_up_kernel(x_ref, wg_ref, wu_ref, o_ref, wg_bf, wu_bf):
    @pl.when(pl.program_id(1) == 0)
    def _():
        wg_bf[...] = wg_ref[...].astype(BF16)
        wu_bf[...] = wu_ref[...].astype(BF16)

    x = x_ref[...]
    g = _dot(x, wg_bf[...])
    u = _dot(x, wu_bf[...])
    o_ref[...] = (g * _sigmoid(g) * u).astype(o_ref.dtype)


def _ffn_up(x, wg, wu, layer, tm, tn):
    m, k = x.shape
    n = wg.shape[2]
    w_spec = pl.BlockSpec((None, k, tn), lambda j, i: (layer, 0, j))
    return pl.pallas_call(
        _ffn_up_kernel, grid=(n // tn, m // tm),
        in_specs=[pl.BlockSpec((tm, k), lambda j, i: (i, 0)), w_spec, w_spec],
        out_specs=pl.BlockSpec((tm, tn), lambda j, i: (i, j)),
        out_shape=jax.ShapeDtypeStruct((m, n), BF16),
        scratch_shapes=[pltpu.VMEM((k, tn), BF16), pltpu.VMEM((k, tn), BF16)],
        compiler_params=_cparams(2), name="ffn_up")(x, wg, wu)


def _mm_res_ln_kernel(a_ref, b_ref, res_ref, g_ref, beta_ref, o_ref, obf_ref, acc_ref):
    k = pl.program_id(1)

    @pl.when(k == 0)
    def _():
        acc_ref[...] = jnp.zeros_like(acc_ref)

    acc_ref[...] += _dot(a_ref[...], b_ref[...].astype(BF16))

    @pl.when(k == pl.num_programs(1) - 1)
    def _():
        y = DN_ALPHA * res_ref[...] + acc_ref[...]
        mu = jnp.mean(y, axis=-1, keepdims=True)
        yc = y - mu
        var = jnp.mean(yc * yc, axis=-1, keepdims=True)
        out = yc * lax.rsqrt(var + LN_EPS) * g_ref[...] + beta_ref[...]
        o_ref[...] = out
        obf_ref[...] = out.astype(BF16)


def _mm_res_ln(a, b, layer, res, g, beta, tm, tk):
    m, k = a.shape
    n = b.shape[2]
    return pl.pallas_call(
        _mm_res_ln_kernel, grid=(m // tm, k // tk),
        in_specs=[pl.BlockSpec((tm, tk), lambda i, j: (i, j)), pl.BlockSpec((None, tk, n), lambda i, j: (layer, j, 0)),
                  pl.BlockSpec((tm, n), lambda i, j: (i, 0)), pl.BlockSpec((1, n), lambda i, j: (0, 0)),
                  pl.BlockSpec((1, n), lambda i, j: (0, 0))],
        out_specs=[pl.BlockSpec((tm, n), lambda i, j: (i, 0)), pl.BlockSpec((tm, n), lambda i, j: (i, 0))],
        out_shape=[jax.ShapeDtypeStruct((m, n), F32), jax.ShapeDtypeStruct((m, n), BF16)],
        scratch_shapes=[pltpu.VMEM((tm, n), F32)],
        compiler_params=_cparams(2), name="mm_res_ln")(a, b, res, g.reshape(1, n), beta.reshape(1, n))


def _merge_kernel(x_ref, wg_ref, bg_ref, br_ref, wb_ref, o_ref, acc_ref):
    n = pl.program_id(2)
    gate = _sigmoid(_dot(x_ref[...], wg_ref[...]) + bg_ref[...])
    contrib = gate * _dot(br_ref[0], wb_ref[...].astype(BF16))

    @pl.when(n == 0)
    def _():
        acc_ref[...] = contrib

    @pl.when(n > 0)
    def _():
        acc_ref[...] += contrib

    @pl.when(n == N_BRANCH - 1)
    def _():
        o_ref[...] = acc_ref[...].astype(o_ref.dtype)


def _merge(x, w_gate, b_gate, branches, w_branch, layer, tm, td):
    m = x.shape[0]
    nd = D_MODEL // td
    return pl.pallas_call(
        _merge_kernel, grid=(m // tm, nd, N_BRANCH),
        in_specs=[pl.BlockSpec((tm, D_MODEL), lambda i, d, n: (i, 0)),
                  pl.BlockSpec((D_MODEL, td), lambda i, d, n: (0, n * nd + d)),
                  pl.BlockSpec((1, td), lambda i, d, n: (0, n * nd + d)),
                  pl.BlockSpec((1, tm, BRANCH_WIDTH), lambda i, d, n: (n, i, 0)),
                  pl.BlockSpec((None, None, BRANCH_WIDTH, td), lambda i, d, n: (layer, n, 0, d))],
        out_specs=pl.BlockSpec((tm, td), lambda i, d, n: (i, d)),
        out_shape=jax.ShapeDtypeStruct((m, D_MODEL), BF16),
        scratch_shapes=[pltpu.VMEM((tm, td), F32)],
        compiler_params=_cparams(3), name="merge")(x, w_gate, b_gate, branches, w_branch)


def _softmax_step(s, m_ref, l_ref, acc_ref, v_bf):
    m_prev = m_ref[...]
    m_new = jnp.maximum(m_prev, jnp.max(s, axis=-1, keepdims=True))
    alpha = jnp.exp(m_prev - m_new)
    p = jnp.exp(s - m_new)
    l_ref[...] = alpha * l_ref[...] + jnp.sum(p, axis=-1, keepdims=True)
    acc_ref[...] = alpha * acc_ref[...] + _dot(p.astype(BF16), v_bf)
    m_ref[...] = m_new


def _mla_prompt_kernel(ql_ref, qr_ref, c_ref, kr_ref, wuv_ref, o_ref, m_ref, l_ref, acc_ref, *, tq, tk):
    qi = pl.program_id(1)
    kj = pl.program_id(2)
    rows = N_HEADS * tq

    @pl.when(kj == 0)
    def _():
        m_ref[...] = jnp.full_like(m_ref, NEG_INF)
        l_ref[...] = jnp.zeros_like(l_ref)
        acc_ref[...] = jnp.zeros_like(acc_ref)

    @pl.when(kj * tk <= qi * tq + tq - 1)
    def _():
        ql = ql_ref[...].reshape(rows, MLA_KV_RANK)
        qr = qr_ref[...].reshape(rows, MLA_ROPE_DIM)
        c = c_ref[...]
        s = (_dot_nt(ql, c) + _dot_nt(qr, kr_ref[...])) * MLA_SCALE
        qpos = qi * tq + (lax.broadcasted_iota(jnp.int32, (rows, tk), 0) & (tq - 1))
        kpos = kj * tk + lax.broadcasted_iota(jnp.int32, (rows, tk), 1)
        s = jnp.where(kpos <= qpos, s, NEG_INF)
        _softmax_step(s, m_ref, l_ref, acc_ref, c)

    @pl.when(kj == pl.num_programs(2) - 1)
    def _():
        o = (acc_ref[...] / l_ref[...]).astype(BF16)
        for h in range(N_HEADS):
            o_ref[:, h * MLA_V_DIM:(h + 1) * MLA_V_DIM] = _dot(o[h * tq:(h + 1) * tq], wuv_ref[h]).astype(o_ref.dtype)


def _mla_prompt(ql, qr, c, kr, wuv, tq=256, tk=512):
    nq = SEQ // tq
    nk = SEQ // tk

    def kv_map(b, i, j):
        return (b * nk + jnp.minimum(j, (i * tq + tq - 1) // tk), 0)

    return pl.pallas_call(
        functools.partial(_mla_prompt_kernel, tq=tq, tk=tk), grid=(BATCH, nq, nk),
        in_specs=[pl.BlockSpec((N_HEADS, tq, MLA_KV_RANK), lambda b, i, j: (0, b * nq + i, 0)),
                  pl.BlockSpec((N_HEADS, tq, MLA_ROPE_DIM), lambda b, i, j: (0, b * nq + i, 0)),
                  pl.BlockSpec((tk, MLA_KV_RANK), kv_map),
                  pl.BlockSpec((tk, MLA_ROPE_DIM), kv_map),
                  pl.BlockSpec((N_HEADS, MLA_KV_RANK, MLA_V_DIM), lambda b, i, j: (0, 0, 0))],
        out_specs=pl.BlockSpec((tq, BRANCH_WIDTH), lambda b, i, j: (b * nq + i, 0)),
        out_shape=jax.ShapeDtypeStruct((BATCH * SEQ, BRANCH_WIDTH), BF16),
        scratch_shapes=[pltpu.VMEM((N_HEADS * tq, 1), F32), pltpu.VMEM((N_HEADS * tq, 1), F32),
                        pltpu.VMEM((N_HEADS * tq, MLA_KV_RANK), F32)],
        compiler_params=_cparams(3), name="mla_prompt")(ql, qr, c, kr, wuv)


def _split_bf16(x):
    hi = x.astype(BF16)
    lo = (x - hi.astype(F32)).astype(BF16)
    return hi, lo


def _topk_select(gate, n_valid, n_cand):
    lane = lax.broadcasted_iota(jnp.int32, gate.shape, 1)
    gm = jnp.where(lane < n_valid, gate, NEG_INF)
    rank = jnp.zeros(gate.shape, jnp.int32)
    for a in range(n_cand):
        ga = gm[:, a:a + 1]
        beats = (ga > gm) | ((ga == gm) & (lane > a))
        rank = rank + beats.astype(jnp.int32)
    return (rank < MOBA_TOPK) & (lane < n_valid)


def _moba_prompt_kernel(q_ref, k_ref, v_ref, o_ref, m_ref, l_ref, acc_ref, *, tq):
    i = pl.program_id(1)
    rows = N_HEADS * tq
    nb = SEQ // MOBA_BLOCK
    qf = jnp.concatenate([q_ref[:, h * MOBA_HEAD_DIM:(h + 1) * MOBA_HEAD_DIM] for h in range(N_HEADS)], axis=0)
    qb = qf.astype(BF16)

    means = jnp.sum(k_ref[...].reshape(nb, MOBA_BLOCK, MOBA_HEAD_DIM), axis=1) * (1.0 / MOBA_BLOCK)
    means = jnp.concatenate([means, jnp.zeros((128 - nb, MOBA_HEAD_DIM), F32)], axis=0)
    q_hi, q_lo = _split_bf16(qf)
    mn_hi, mn_lo = _split_bf16(means)
    gate = _dot_nt(q_hi, mn_hi) + _dot_nt(q_hi, mn_lo) + _dot_nt(q_lo, mn_hi)
    sel = _topk_select(gate, i, nb).astype(F32)
    lane = lax.broadcasted_iota(jnp.int32, (rows, 128), 1)

    own0 = pl.multiple_of(i * MOBA_BLOCK, MOBA_BLOCK)
    s = _dot_nt(qb, k_ref[pl.ds(own0, MOBA_BLOCK), :].astype(BF16)) * MOBA_SCALE
    qrel = lax.broadcasted_iota(jnp.int32, (rows, MOBA_BLOCK), 0) & (tq - 1)
    krel = lax.broadcasted_iota(jnp.int32, (rows, MOBA_BLOCK), 1)
    s = jnp.where(krel <= qrel, s, NEG_INF)
    m_ref[...] = jnp.full_like(m_ref, NEG_INF)
    l_ref[...] = jnp.zeros_like(l_ref)
    acc_ref[...] = jnp.zeros_like(acc_ref)
    _softmax_step(s, m_ref, l_ref, acc_ref, v_ref[pl.ds(own0, MOBA_BLOCK), :].astype(BF16))

    def past_block(j, carry):
        j0 = pl.multiple_of(j * MOBA_BLOCK, MOBA_BLOCK)
        sj = _dot_nt(qb, k_ref[pl.ds(j0, MOBA_BLOCK), :].astype(BF16)) * MOBA_SCALE
        picked = jnp.max(jnp.where(lane == j, sel, 0.0), axis=-1, keepdims=True) > 0.0
        sj = jnp.where(picked, sj, NEG_INF)
        _softmax_step(sj, m_ref, l_ref, acc_ref, v_ref[pl.ds(j0, MOBA_BLOCK), :].astype(BF16))
        return carry

    lax.fori_loop(0, i, past_block, 0)
    o = acc_ref[...] / l_ref[...]
    for h in range(N_HEADS):
        o_ref[:, h * MOBA_HEAD_DIM:(h + 1) * MOBA_HEAD_DIM] = o[h * tq:(h + 1) * tq].astype(o_ref.dtype)


def _moba_prompt(proj):
    tq = MOBA_BLOCK
    nq = SEQ // tq
    return pl.pallas_call(
        functools.partial(_moba_prompt_kernel, tq=tq), grid=(BATCH, nq),
        in_specs=[pl.BlockSpec((tq, 512), lambda b, i: (b * nq + i, 1536 // 512)),
                  pl.BlockSpec((SEQ, 128), lambda b, i: (b, 2048 // 128)),
                  pl.BlockSpec((SEQ, 128), lambda b, i: (b, 2176 // 128))],
        out_specs=pl.BlockSpec((tq, BRANCH_WIDTH), lambda b, i: (b * nq + i, 0)),
        out_shape=jax.ShapeDtypeStruct((BATCH * SEQ, BRANCH_WIDTH), BF16),
        scratch_shapes=[pltpu.VMEM((N_HEADS * tq, 1), F32), pltpu.VMEM((N_HEADS * tq, 1), F32),
                        pltpu.VMEM((N_HEADS * tq, MOBA_HEAD_DIM), F32)],
        compiler_params=_cparams(2), name="moba_prompt")(proj, proj, proj)


def _ret_log_gamma():
    return np.log1p(-np.exp2(-5.0 - np.arange(RET_HEADS, dtype=np.float32))).astype(np.float32)


def _ret_tables(chunk):
    log_g = jnp.asarray(_ret_log_gamma())
    idx = jnp.arange(chunk, dtype=F32)
    diff = idx[:, None] - idx[None, :]
    decay = jnp.where(diff >= 0, jnp.exp(jnp.maximum(diff, 0.0)[None] * log_g[:, None, None]), 0.0)
    q_dec = jnp.exp((idx + 1.0)[None, :] * log_g[:, None])[..., None]
    k_dec = jnp.exp((chunk - 1.0 - idx)[None, :] * log_g[:, None])[:, None, :]
    c_dec = [float(np.exp(np.float32(chunk) * g)) for g in _ret_log_gamma()]
    return decay, q_dec, k_dec, c_dec


def _ret_prompt_kernel(q_ref, kt_ref, v_ref, dec_ref, qd_ref, kd_ref, o_ref, s_ref, st_ref, *, c_dec):
    c = pl.program_id(1)

    @pl.when(c == 0)
    def _():
        st_ref[...] = jnp.zeros_like(st_ref)

    for h in range(RET_HEADS):
        q = q_ref[h]
        kt = kt_ref[h]
        v = v_ref[h].astype(BF16)
        state = st_ref[h]
        att = _dot(q.astype(BF16), kt.astype(BF16)) * dec_ref[h]
        o = _dot(att.astype(BF16), v) + _dot((q * qd_ref[h]).astype(BF16), state.astype(BF16))
        st_ref[h] = state * c_dec[h] + _dot((kt * kd_ref[h]).astype(BF16), v)
        o_ref[:, h * RET_VAL_DIM:(h + 1) * RET_VAL_DIM] = o

    @pl.when(c == pl.num_programs(1) - 1)
    def _():
        s_ref[0] = st_ref[...]


def _ret_prompt(q, kt, v, chunk=256):
    nc = SEQ // chunk
    decay, q_dec, k_dec, c_dec = _ret_tables(chunk)
    return pl.pallas_call(
        functools.partial(_ret_prompt_kernel, c_dec=c_dec), grid=(BATCH, nc),
        in_specs=[pl.BlockSpec((RET_HEADS, chunk, RET_KEY_DIM), lambda b, c: (0, b * nc + c, 0)),
                  pl.BlockSpec((RET_HEADS, RET_KEY_DIM, chunk), lambda b, c: (0, 0, b * nc + c)),
                  pl.BlockSpec((RET_HEADS, chunk, RET_VAL_DIM), lambda b, c: (0, b * nc + c, 0)),
                  pl.BlockSpec((RET_HEADS, chunk, chunk), lambda b, c: (0, 0, 0)),
                  pl.BlockSpec((RET_HEADS, chunk, 1), lambda b, c: (0, 0, 0)),
                  pl.BlockSpec((RET_HEADS, 1, chunk), lambda b, c: (0, 0, 0))],
        out_specs=[pl.BlockSpec((chunk, BRANCH_WIDTH), lambda b, c: (b * nc + c, 0)),
                   pl.BlockSpec((1, RET_HEADS, RET_KEY_DIM, RET_VAL_DIM), lambda b, c: (b, 0, 0, 0))],
        out_shape=[jax.ShapeDtypeStruct((BATCH * SEQ, BRANCH_WIDTH), F32),
                   jax.ShapeDtypeStruct((BATCH, RET_HEADS, RET_KEY_DIM, RET_VAL_DIM), F32)],
        scratch_shapes=[pltpu.VMEM((RET_HEADS, RET_KEY_DIM, RET_VAL_DIM), F32)],
        compiler_params=_cparams(2), name="ret_prompt")(q, kt, v, decay, q_dec, k_dec)


GLA_C = 64
GLA_SUB = 16


def _gla_chunk(q, k, v, la, state_t):
    c = GLA_C
    row = lax.broadcasted_iota(jnp.int32, (c, c), 0)
    col = lax.broadcasted_iota(jnp.int32, (c, c), 1)
    tril = (col <= row).astype(BF16)
    la_hi = la.astype(BF16)
    r1 = la - la_hi.astype(F32)
    la_mid = r1.astype(BF16)
    la_lo = (r1 - la_mid.astype(F32)).astype(BF16)
    b = _dot(tril, la_hi) + _dot(tril, la_mid) + _dot(tril, la_lo)
    b_last = b[c - 1:c, :]
    v_bf = v.astype(BF16)
    o = _dot_nt((q * jnp.exp(b)).astype(BF16), state_t.astype(BF16))
    k_state = (k * jnp.exp(b_last - b)).astype(BF16)
    new_state_t = state_t * jnp.exp(b_last) + _dot_tn(v_bf, k_state)

    rsub = lax.broadcasted_iota(jnp.int32, (c, GLA_KEY_DIM), 0)
    tri3 = (lax.broadcasted_iota(jnp.int32, (GLA_SUB, GLA_SUB, GLA_KEY_DIM), 1)
            <= lax.broadcasted_iota(jnp.int32, (GLA_SUB, GLA_SUB, GLA_KEY_DIM), 0))
    o_rows = []
    for blk in range(c // GLA_SUB):
        r0 = blk * GLA_SUB
        qi, ki, bi = q[r0:r0 + GLA_SUB], k[r0:r0 + GLA_SUB], b[r0:r0 + GLA_SUB]
        rel = jnp.where(tri3, jnp.exp(jnp.minimum(bi[:, None, :] - bi[None, :, :], 0.0)), 0.0)
        att_d = jnp.sum(qi[:, None, :] * ki[None, :, :] * rel, axis=-1)
        o_blk = o[r0:r0 + GLA_SUB] + _dot(att_d.astype(BF16), v_bf[r0:r0 + GLA_SUB])
        if blk > 0:
            ref_b = b[r0 - 1:r0, :]
            q_off = (qi * jnp.exp(bi - ref_b)).astype(BF16)
            k_off = jnp.where(rsub < r0, k * jnp.exp(jnp.minimum(ref_b - b, 0.0)), 0.0).astype(BF16)
            o_blk = o_blk + _dot(_dot_nt(q_off, k_off).astype(BF16), v_bf)
        o_rows.append(o_blk)
    return jnp.concatenate(o_rows, axis=0), new_state_t


def _gla_prompt_kernel(q_ref, k_ref, v_ref, la_ref, o_ref, s_ref, st_ref, *, n_chunks):
    step = pl.program_id(1)

    @pl.when(step == 0)
    def _():
        st_ref[...] = jnp.zeros_like(st_ref)

    for h in range(GLA_HEADS):
        def chunk_body(ci, carry, h=h):
            r0 = pl.multiple_of(ci * GLA_C, GLA_C)
            o, new_state = _gla_chunk(q_ref[h, pl.ds(r0, GLA_C), :], k_ref[h, pl.ds(r0, GLA_C), :],
                                      v_ref[h, pl.ds(r0, GLA_C), :],
                                      la_ref[h, pl.ds(r0, GLA_C), :], st_ref[h])
            st_ref[h] = new_state
            o_ref[pl.ds(r0, GLA_C), h * GLA_VAL_DIM:(h + 1) * GLA_VAL_DIM] = o
            return carry

        lax.fori_loop(0, n_chunks, chunk_body, 0)

    @pl.when(step == pl.num_programs(1) - 1)
    def _():
        s_ref[0] = st_ref[...]


def _gla_prompt(q, k, v, la, tile=256):
    nt = SEQ // tile
    row_spec = lambda d: pl.BlockSpec((GLA_HEADS, tile, d), lambda b, c: (0, b * nt + c, 0))
    return pl.pallas_call(
        functools.partial(_gla_prompt_kernel, n_chunks=tile // GLA_C), grid=(BATCH, nt),
        in_specs=[row_spec(GLA_KEY_DIM), row_spec(GLA_KEY_DIM), row_spec(GLA_VAL_DIM), row_spec(GLA_KEY_DIM)],
        out_specs=[pl.BlockSpec((tile, BRANCH_WIDTH), lambda b, c: (b * nt + c, 0)),
                   pl.BlockSpec((1, GLA_HEADS, GLA_VAL_DIM, GLA_KEY_DIM), lambda b, c: (b, 0, 0, 0))],
        out_shape=[jax.ShapeDtypeStruct((BATCH * SEQ, BRANCH_WIDTH), F32),
                   jax.ShapeDtypeStruct((BATCH, GLA_HEADS, GLA_VAL_DIM, GLA_KEY_DIM), F32)],
        scratch_shapes=[pltpu.VMEM((GLA_HEADS, GLA_VAL_DIM, GLA_KEY_DIM), F32)],
        compiler_params=_cparams(2), name="gla_prompt")(q, k, v, la)


SAMPLE_NB = 8


def _ret_sample_kernel(q_ref, kt_ref, v_ref, s0_ref, dec_ref, qd_ref, kd_ref, cd_ref, o_ref, s_ref):
    nb = SAMPLE_NB * RET_HEADS
    q = q_ref[...].reshape(nb, DEC_SEQ, RET_KEY_DIM)
    kt = kt_ref[...].reshape(nb, RET_KEY_DIM, DEC_SEQ)
    v = v_ref[...].reshape(nb, DEC_SEQ, RET_VAL_DIM).astype(BF16)
    s0 = s0_ref[...].reshape(nb, RET_KEY_DIM, RET_VAL_DIM)
    att = jnp.einsum('bqd,bdk->bqk', q.astype(BF16), kt.astype(BF16), preferred_element_type=F32) * dec_ref[...]
    o = (jnp.einsum('bqk,bkv->bqv', att.astype(BF16), v, preferred_element_type=F32)
         + jnp.einsum('bqd,bdv->bqv', (q * qd_ref[...]).astype(BF16), s0.astype(BF16), preferred_element_type=F32))
    s_new = s0 * cd_ref[...] + jnp.einsum('bdt,btv->bdv', (kt * kd_ref[...]).astype(BF16), v, preferred_element_type=F32)
    o_ref[...] = o.reshape(SAMPLE_NB, RET_HEADS, DEC_SEQ, RET_VAL_DIM)
    s_ref[...] = s_new.reshape(SAMPLE_NB, RET_HEADS, RET_KEY_DIM, RET_VAL_DIM)


def _ret_sample(q, kt, v, s0):
    decay, q_dec, k_dec, c_dec = _ret_tables(DEC_SEQ)
    rep = lambda t: jnp.tile(t, (SAMPLE_NB,) + (1,) * (t.ndim - 1))
    cd = rep(jnp.asarray(c_dec, F32).reshape(RET_HEADS, 1, 1))
    blk = lambda *dims: pl.BlockSpec((SAMPLE_NB, RET_HEADS) + dims, lambda i: (i, 0, 0, 0))
    full = lambda t: pl.BlockSpec(t.shape, lambda i: (0, 0, 0))
    tabs = [rep(decay), rep(q_dec), rep(k_dec), cd]
    return pl.pallas_call(
        _ret_sample_kernel, grid=(DEC_BATCH // SAMPLE_NB,),
        in_specs=[blk(DEC_SEQ, RET_KEY_DIM), blk(RET_KEY_DIM, DEC_SEQ), blk(DEC_SEQ, RET_VAL_DIM),
                  blk(RET_KEY_DIM, RET_VAL_DIM)] + [full(t) for t in tabs],
        out_specs=[blk(DEC_SEQ, RET_VAL_DIM), blk(RET_KEY_DIM, RET_VAL_DIM)],
        out_shape=[jax.ShapeDtypeStruct((DEC_BATCH, RET_HEADS, DEC_SEQ, RET_VAL_DIM), F32),
                   jax.ShapeDtypeStruct((DEC_BATCH, RET_HEADS, RET_KEY_DIM, RET_VAL_DIM), F32)],
        compiler_params=_cparams(1), name="ret_sample")(q, kt, v, s0, *tabs)


def _gla_sample_kernel(q_ref, k_ref, kt_ref, v_ref, la_ref, lat_ref, s0_ref, o_ref, s_ref):
    nb = SAMPLE_NB * GLA_HEADS
    t_n = DEC_SEQ
    q = q_ref[...].reshape(nb, t_n, GLA_KEY_DIM)
    k = k_ref[...].reshape(nb, t_n, GLA_KEY_DIM)
    v = v_ref[...].reshape(nb, t_n, GLA_VAL_DIM)
    la = la_ref[...].reshape(nb, t_n, GLA_KEY_DIM)
    kt = kt_ref[...].reshape(nb, GLA_KEY_DIM, t_n)
    lat = lat_ref[...].reshape(nb, GLA_KEY_DIM, t_n)
    s0 = s0_ref[...].reshape(nb, GLA_KEY_DIM, GLA_VAL_DIM)
    b_rows, bt_cols = [], []
    run, run_t = None, None
    for t in range(t_n):
        run = la[:, t, :] if run is None else run + la[:, t, :]
        run_t = lat[:, :, t:t + 1] if run_t is None else run_t + lat[:, :, t:t + 1]
        b_rows.append(run)
        bt_cols.append(run_t)
    b_last_t = bt_cols[-1]
    q_state = jnp.stack([q[:, t, :] * jnp.exp(b_rows[t]) for t in range(t_n)], axis=1)
    o = jnp.einsum('bqd,bdv->bqv', q_state.astype(BF16), s0.astype(BF16), preferred_element_type=F32)
    o_rows = []
    for i in range(t_n):
        o_i = o[:, i, :]
        for j in range(i + 1):
            w = jnp.sum(q[:, i, :] * k[:, j, :] * jnp.exp(jnp.minimum(b_rows[i] - b_rows[j], 0.0)),
                        axis=-1, keepdims=True)
            o_i = o_i + w * v[:, j, :]
        o_rows.append(o_i)
    k_state_t = jnp.concatenate([kt[:, :, t:t + 1] * jnp.exp(b_last_t - bt_cols[t]) for t in range(t_n)], axis=2)
    s_new = s0 * jnp.exp(b_last_t) + jnp.einsum('bdt,btv->bdv', k_state_t.astype(BF16), v.astype(BF16),
                                                  preferred_element_type=F32)
    o_ref[...] = jnp.stack(o_rows, axis=1).reshape(SAMPLE_NB, GLA_HEADS, t_n, GLA_VAL_DIM)
    s_ref[...] = s_new.reshape(SAMPLE_NB, GLA_HEADS, GLA_KEY_DIM, GLA_VAL_DIM)


def _gla_sample(q, k, v, la, s0):
    kt = jnp.swapaxes(k, 2, 3)
    lat = jnp.swapaxes(la, 2, 3)
    blk = lambda *dims: pl.BlockSpec((SAMPLE_NB, GLA_HEADS) + dims, lambda i: (i, 0, 0, 0))
    return pl.pallas_call(
        _gla_sample_kernel, grid=(DEC_BATCH // SAMPLE_NB,),
        in_specs=[blk(DEC_SEQ, GLA_KEY_DIM), blk(DEC_SEQ, GLA_KEY_DIM), blk(GLA_KEY_DIM, DEC_SEQ),
                  blk(DEC_SEQ, GLA_VAL_DIM), blk(DEC_SEQ, GLA_KEY_DIM), blk(GLA_KEY_DIM, DEC_SEQ),
                  blk(GLA_KEY_DIM, GLA_VAL_DIM)],
        out_specs=[blk(DEC_SEQ, GLA_VAL_DIM), blk(GLA_KEY_DIM, GLA_VAL_DIM)],
        out_shape=[jax.ShapeDtypeStruct((DEC_BATCH, GLA_HEADS, DEC_SEQ, GLA_VAL_DIM), F32),
                   jax.ShapeDtypeStruct((DEC_BATCH, GLA_HEADS, GLA_KEY_DIM, GLA_VAL_DIM), F32)],
        compiler_params=_cparams(1), name="gla_sample")(q, k, kt, v, la, lat, s0)


Q_ROWS = N_HEADS * DEC_SEQ
MLA_PAGES = 64
MLA_KEYS = MLA_PAGES * PAGE_SIZE
MLA_STEPS = N_PAGES // MLA_PAGES


def _new_token_mask():
    tok = lax.broadcasted_iota(jnp.int32, (Q_ROWS, DEC_SEQ), 0) & (DEC_SEQ - 1)
    key = lax.broadcasted_iota(jnp.int32, (Q_ROWS, DEC_SEQ), 1)
    return key <= tok


def _mla_sample_kernel(pt_ref, ql_ref, qr_ref, cn_ref, kn_ref, lat_hbm, rope_hbm, o_ref,
                       lat_buf, rope_buf, lat_sem, rope_sem, m_ref, l_ref, acc_ref, *, layer):
    s = pl.program_id(0)
    c = pl.program_id(1)
    g = s * MLA_STEPS + c
    slot = lax.rem(g, 2)

    def copies(seq, chunk, slot_):
        out = []
        for i in range(MLA_PAGES):
            page = pt_ref[seq, chunk * MLA_PAGES + i]
            out.append(pltpu.make_async_copy(lat_hbm.at[layer, page], lat_buf.at[slot_, pl.ds(i * PAGE_SIZE, PAGE_SIZE)],
                                             lat_sem.at[slot_]))
            out.append(pltpu.make_async_copy(rope_hbm.at[layer, page], rope_buf.at[slot_, i], rope_sem.at[slot_]))
        return out

    @pl.when(g == 0)
    def _():
        for cp in copies(0, 0, 0):
            cp.start()

    @pl.when(g + 1 < DEC_BATCH * MLA_STEPS)
    def _():
        nxt = g + 1
        for cp in copies(nxt // MLA_STEPS, lax.rem(nxt, MLA_STEPS), 1 - slot):
            cp.start()

    for cp in copies(s, c, slot):
        cp.wait()

    @pl.when(c == 0)
    def _():
        m_ref[...] = jnp.full_like(m_ref, NEG_INF)
        l_ref[...] = jnp.zeros_like(l_ref)
        acc_ref[...] = jnp.zeros_like(acc_ref)

    ql = ql_ref[0].astype(BF16)
    qr = qr_ref[0].astype(BF16)
    lat = lat_buf[slot].astype(BF16)
    s_rope = jnp.concatenate([_dot(qr, rope_buf[slot, i].astype(BF16)) for i in range(MLA_PAGES)], axis=1)
    sc = (_dot_nt(ql, lat) + s_rope) * MLA_SCALE
    _softmax_step(sc, m_ref, l_ref, acc_ref, lat)

    @pl.when(c == MLA_STEPS - 1)
    def _():
        cn = cn_ref[0].astype(BF16)
        sn = (_dot_nt(ql, cn) + _dot_nt(qr, kn_ref[0].astype(BF16))) * MLA_SCALE
        sn = jnp.where(_new_token_mask(), sn, NEG_INF)
        _softmax_step(sn, m_ref, l_ref, acc_ref, cn)
        o_ref[0] = acc_ref[...] / l_ref[...]


def _mla_sample(page_table, ql, qr, cn, kn, cache_lat, cache_rope_t, layer):
    seq_blk = lambda r, d: pl.BlockSpec((1, r, d), lambda s, c, pt: (s, 0, 0))
    grid_spec = pltpu.PrefetchScalarGridSpec(
        num_scalar_prefetch=1, grid=(DEC_BATCH, MLA_STEPS),
        in_specs=[seq_blk(Q_ROWS, MLA_KV_RANK), seq_blk(Q_ROWS, MLA_ROPE_DIM), seq_blk(DEC_SEQ, MLA_KV_RANK),
                  seq_blk(DEC_SEQ, MLA_ROPE_DIM), pl.BlockSpec(memory_space=pl.ANY), pl.BlockSpec(memory_space=pl.ANY)],
        out_specs=seq_blk(Q_ROWS, MLA_KV_RANK),
        scratch_shapes=[pltpu.VMEM((2, MLA_KEYS, MLA_KV_RANK), F32),
                        pltpu.VMEM((2, MLA_PAGES, MLA_ROPE_DIM, PAGE_SIZE), F32),
                        pltpu.SemaphoreType.DMA((2,)), pltpu.SemaphoreType.DMA((2,)),
                        pltpu.VMEM((Q_ROWS, 1), F32), pltpu.VMEM((Q_ROWS, 1), F32), pltpu.VMEM((Q_ROWS, MLA_KV_RANK), F32)])
    return pl.pallas_call(
        functools.partial(_mla_sample_kernel, layer=layer), grid_spec=grid_spec,
        out_shape=jax.ShapeDtypeStruct((DEC_BATCH, Q_ROWS, MLA_KV_RANK), F32),
        compiler_params=_cparams(2), name="mla_sample")(page_table, ql, qr, cn, kn, cache_lat, cache_rope_t)


N_MOBA_BLOCKS = PAST_LEN // MOBA_BLOCK
MOBA_CHUNK = 4096
MOBA_CHUNKS = PAST_LEN // MOBA_CHUNK
BLOCKS_PER_CHUNK = MOBA_CHUNK // MOBA_BLOCK


def _moba_sample_kernel(pt_ref, q_ref, kn_ref, vn_ref, k_hbm, v_hbm, o_ref, k_buf, v_buf, k_sem, v_sem, s_ref, p_ref,
                        *, layer):
    s = pl.program_id(0)
    slot = lax.rem(s, 2)

    def copies(hbm, buf, sem, seq, slot_):
        return [pltpu.make_async_copy(hbm.at[layer, pt_ref[seq, i]], buf.at[slot_, pl.ds(i * PAGE_SIZE, PAGE_SIZE)],
                                      sem.at[slot_]) for i in range(N_PAGES)]

    def start(seq, slot_):
        for cp in copies(k_hbm, k_buf, k_sem, seq, slot_):
            cp.start()
        for cp in copies(v_hbm, v_buf, v_sem, seq, slot_):
            cp.start()

    @pl.when(s == 0)
    def _():
        start(0, 0)

    @pl.when(s + 1 < DEC_BATCH)
    def _():
        start(s + 1, 1 - slot)

    for cp in copies(k_hbm, k_buf, k_sem, s, slot):
        cp.wait()

    qb = q_ref[0].astype(BF16)
    lane = lax.broadcasted_iota(jnp.int32, (Q_ROWS, 128), 1)
    gate = jnp.zeros((Q_ROWS, 128), F32)
    for ci in range(MOBA_CHUNKS):
        raw = _dot_nt(qb, k_buf[slot, ci * MOBA_CHUNK:(ci + 1) * MOBA_CHUNK, :].astype(BF16))
        s_ref[ci] = raw
        for blk in range(BLOCKS_PER_CHUNK):
            tot = jnp.sum(raw[:, blk * MOBA_BLOCK:(blk + 1) * MOBA_BLOCK], axis=-1, keepdims=True)
            gate = jnp.where(lane == ci * BLOCKS_PER_CHUNK + blk, tot * (1.0 / MOBA_BLOCK), gate)

    sel = _topk_select(gate, N_MOBA_BLOCKS, N_MOBA_BLOCKS).astype(BF16)
    sn = jnp.where(_new_token_mask(), _dot_nt(qb, kn_ref[0].astype(BF16)) * MOBA_SCALE, NEG_INF)
    m = jnp.max(sn, axis=-1, keepdims=True)
    key_blk = lax.broadcasted_iota(jnp.int32, (128, MOBA_CHUNK), 1) // MOBA_BLOCK
    blk_row = lax.broadcasted_iota(jnp.int32, (128, MOBA_CHUNK), 0)
    masked = []
    for ci in range(MOBA_CHUNKS):
        expand = (blk_row == key_blk + ci * BLOCKS_PER_CHUNK).astype(BF16)
        picked = _dot(sel, expand) > 0.5
        sc = jnp.where(picked, s_ref[ci] * MOBA_SCALE, NEG_INF)
        masked.append(sc)
        m = jnp.maximum(m, jnp.max(sc, axis=-1, keepdims=True))
    pn = jnp.exp(sn - m)
    l = jnp.sum(pn, axis=-1, keepdims=True)
    for ci in range(MOBA_CHUNKS):
        p = jnp.exp(masked[ci] - m)
        l = l + jnp.sum(p, axis=-1, keepdims=True)
        p_ref[ci] = p.astype(BF16)

    for cp in copies(v_hbm, v_buf, v_sem, s, slot):
        cp.wait()

    acc = _dot(pn.astype(BF16), vn_ref[0].astype(BF16))
    for ci in range(MOBA_CHUNKS):
        acc = acc + _dot(p_ref[ci], v_buf[slot, ci * MOBA_CHUNK:(ci + 1) * MOBA_CHUNK, :].astype(BF16))
    o_ref[0] = acc / l


def _moba_sample(page_table, q, kn, vn, cache_k, cache_v, layer):
    seq_blk = lambda r, d: pl.BlockSpec((1, r, d), lambda s, pt: (s, 0, 0))
    grid_spec = pltpu.PrefetchScalarGridSpec(
        num_scalar_prefetch=1, grid=(DEC_BATCH,),
        in_specs=[seq_blk(Q_ROWS, MOBA_HEAD_DIM), seq_blk(DEC_SEQ, MOBA_HEAD_DIM), seq_blk(DEC_SEQ, MOBA_HEAD_DIM),
                  pl.BlockSpec(memory_space=pl.ANY), pl.BlockSpec(memory_space=pl.ANY)],
        out_specs=seq_blk(Q_ROWS, MOBA_HEAD_DIM),
        scratch_shapes=[pltpu.VMEM((2, PAST_LEN, MOBA_HEAD_DIM), F32), pltpu.VMEM((2, PAST_LEN, MOBA_HEAD_DIM), F32),
                        pltpu.SemaphoreType.DMA((2,)), pltpu.SemaphoreType.DMA((2,)),
                        pltpu.VMEM((MOBA_CHUNKS, Q_ROWS, MOBA_CHUNK), F32),
                        pltpu.VMEM((MOBA_CHUNKS, Q_ROWS, MOBA_CHUNK), BF16)])
    return pl.pallas_call(
        functools.partial(_moba_sample_kernel, layer=layer), grid_spec=grid_spec,
        out_shape=jax.ShapeDtypeStruct((DEC_BATCH, Q_ROWS, MOBA_HEAD_DIM), F32),
        compiler_params=_cparams(1), name="moba_sample")(page_table, q, kn, vn, cache_k, cache_v)


def _pack_w_main(w):
    pad = jnp.zeros((D_MODEL, 48), w.dtype)
    return jnp.concatenate([w[:, :2944], w[:, 2944:3008], w[:, 4032:4048], pad, w[:, 3008:4032], w[:, 4048:4560]],
                           axis=1).astype(BF16)


def _rms(x, g):
    return x * lax.rsqrt(jnp.mean(x * x, -1, keepdims=True) + RMS_EPS) * g


def _rope(x, pos, inv_freq):
    ang = pos.astype(F32)[:, None] * inv_freq[None, :]
    shape = (x.shape[0],) + (1,) * (x.ndim - 2) + (inv_freq.shape[0],)
    cos = jnp.cos(ang).reshape(shape)
    sin = jnp.sin(ang).reshape(shape)
    x1, x2 = jnp.split(x, 2, axis=-1)
    return jnp.concatenate([x1 * cos - x2 * sin, x2 * cos + x1 * sin], axis=-1)


def _heads_major(t):
    return jnp.swapaxes(t, 0, 1)


def _layer_weights(l, W):
    w_uq = W['mla_w_uq'][l]
    return dict(
        w_main=_pack_w_main(W['w_in'][l]),
        w_gate=W['w_in'][l][:, GATE_COL0:].astype(BF16),
        b_gate=W['b_gate'][l].reshape(1, N_BRANCH * D_MODEL),
        w_uq=jnp.concatenate([w_uq[:, :, :MLA_NOPE_DIM].reshape(MLA_Q_RANK, -1),
                              w_uq[:, :, MLA_NOPE_DIM:].reshape(MLA_Q_RANK, -1)], axis=1).astype(BF16),
        w_uk_t=jnp.transpose(W['mla_w_uk'][l], (1, 2, 0)).astype(BF16),
        w_uv=jnp.transpose(W['mla_w_uv'][l], (1, 0, 2)).astype(BF16),
    )


def _project(x_bf, pos, l, W, LW, tm):
    m = x_bf.shape[0]
    proj = _matmul(x_bf, LW['w_main'], tm, 512, F32)
    ret_inv = 1.0 / (ROPE_THETA ** jnp.linspace(0.0, 1.0, RET_KEY_DIM // 2, dtype=F32))
    mla_inv = 1.0 / (ROPE_THETA ** (jnp.arange(0, MLA_ROPE_DIM, 2, dtype=F32) / MLA_ROPE_DIM))
    p = {'proj': proj}
    p['ret_q'] = _rope(proj[:, 0:256].reshape(m, RET_HEADS, RET_KEY_DIM), pos, ret_inv)
    p['ret_k'] = _rope(proj[:, 256:512].reshape(m, RET_HEADS, RET_KEY_DIM), pos, ret_inv) * (RET_KEY_DIM ** -0.5)
    p['ret_v'] = proj[:, 512:1024].reshape(m, RET_HEADS, RET_VAL_DIM)
    p['ret_g'] = proj[:, 1024:1536]
    p['moba_q'] = proj[:, 1536:2048]
    p['moba_k'] = proj[:, 2048:2176]
    p['moba_v'] = proj[:, 2176:2304]
    cq = _rms(proj[:, 2304:2688], W['mla_q_norm_g'][l])
    qm = _matmul(cq.astype(BF16), LW['w_uq'], tm, 768, F32)
    q_nope = _heads_major(qm[:, :512].reshape(m, MLA_HEADS, MLA_NOPE_DIM)).astype(BF16)
    p['mla_q_lat'] = _matmul_heads(q_nope, LW['w_uk_t'], tm, F32)
    p['mla_q_rope'] = _heads_major(_rope(qm[:, 512:].reshape(m, MLA_HEADS, MLA_ROPE_DIM), pos, mla_inv))
    p['mla_c'] = _rms(proj[:, 2688:2944], W['mla_kv_norm_g'][l])
    p['mla_kr'] = _rope(proj[:, 2944:3008], pos, mla_inv)
    ga = proj[:, 3008:3024]
    p['gla_q'] = proj[:, 3072:3328].reshape(m, GLA_HEADS, GLA_KEY_DIM) * (GLA_KEY_DIM ** -0.5)
    p['gla_k'] = proj[:, 3328:3584].reshape(m, GLA_HEADS, GLA_KEY_DIM)
    p['gla_v'] = proj[:, 3584:4096].reshape(m, GLA_HEADS, GLA_VAL_DIM)
    a_logit = _matmul(ga.astype(BF16), W['gla_w_a'][l].reshape(GLA_GATE_RANK, -1).astype(BF16), tm, 256, F32)
    a_logit = a_logit.reshape(m, GLA_HEADS, GLA_KEY_DIM) + W['gla_b_a'][l]
    p['gla_log_a'] = jax.nn.log_sigmoid(a_logit) / GLA_GATE_TAU
    p['gla_r'] = proj[:, 4096:4608]
    return p


def _finish_layer(x, x_bf, p, l, W, LW, ret_o, moba_o, mla_o, gla_o, tm):
    m = x.shape[0]
    ret = (_rms(ret_o.reshape(m, RET_HEADS, RET_VAL_DIM), W['ret_norm_g'][l]).reshape(m, BRANCH_WIDTH)
           * jax.nn.silu(p['ret_g']))
    gla = (_rms(gla_o.reshape(m, GLA_HEADS, GLA_VAL_DIM), W['gla_norm_g'][l]).reshape(m, BRANCH_WIDTH)
           * jax.nn.silu(p['gla_r']))
    branches = jnp.stack([ret.astype(BF16), moba_o.astype(BF16), mla_o.astype(BF16), gla.astype(BF16)], axis=0)
    merged = _merge(x_bf, LW['w_gate'], LW['b_gate'], branches, W['w_branch'], l, tm, 512)
    x1, x1_bf = _mm_res_ln(merged, W['w_out'], l, x, W['ln1_g'][l], W['ln1_b'][l], 512, 512)
    hidden = _ffn_up(x1_bf, W['ffn_w_gate'], W['ffn_w_up'], l, tm, 512)
    return _mm_res_ln(hidden, W['ffn_w_down'], l, x1, W['ln2_g'][l], W['ln2_b'][l], 512, 512)


def _prompt_layer(x, x_bf, l, W, LW):
    m = BATCH * SEQ
    pos = jnp.tile(jnp.arange(SEQ, dtype=jnp.int32), BATCH)
    p = _project(x_bf, pos, l, W, LW, 1024)
    ret_o, ret_s = _ret_prompt(_heads_major(p['ret_q']), jnp.transpose(p['ret_k'], (1, 2, 0)), _heads_major(p['ret_v']))
    gla_o, gla_st = _gla_prompt(_heads_major(p['gla_q']), _heads_major(p['gla_k']), _heads_major(p['gla_v']),
                                _heads_major(p['gla_log_a']))
    moba_o = _moba_prompt(p['proj'])
    mla_o = _mla_prompt(p['mla_q_lat'].astype(BF16), p['mla_q_rope'].astype(BF16), p['mla_c'].astype(BF16),
                        p['mla_kr'].astype(BF16), LW['w_uv'])
    x, x_bf = _finish_layer(x, x_bf, p, l, W, LW, ret_o, moba_o, mla_o, gla_o, 1024)
    new = (p['moba_k'].reshape(BATCH, SEQ, 1, MOBA_HEAD_DIM), p['moba_v'].reshape(BATCH, SEQ, 1, MOBA_HEAD_DIM),
           p['mla_c'].reshape(BATCH, SEQ, MLA_KV_RANK), p['mla_kr'].reshape(BATCH, SEQ, MLA_ROPE_DIM),
           ret_s, jnp.swapaxes(gla_st, 2, 3))
    return x, x_bf, new


def _sample_layer(x, x_bf, l, W, LW, caches, page_table):
    m = DEC_BATCH * DEC_SEQ
    pos = jnp.tile(PAST_LEN + jnp.arange(DEC_SEQ, dtype=jnp.int32), DEC_BATCH)
    p = _project(x_bf, pos, l, W, LW, 512)
    seq_heads = lambda t, d: jnp.swapaxes(t.reshape(DEC_BATCH, DEC_SEQ, N_HEADS, d), 1, 2)
    back = lambda t, d: jnp.swapaxes(t, 1, 2).reshape(m, N_HEADS * d)

    ret_k = seq_heads(p['ret_k'], RET_KEY_DIM)
    ret_o, ret_s = _ret_sample(seq_heads(p['ret_q'], RET_KEY_DIM), jnp.swapaxes(ret_k, 2, 3),
                               seq_heads(p['ret_v'], RET_VAL_DIM), caches['state_retention'][l])
    gla_o, gla_s = _gla_sample(seq_heads(p['gla_q'], GLA_KEY_DIM), seq_heads(p['gla_k'], GLA_KEY_DIM),
                               seq_heads(p['gla_v'], GLA_VAL_DIM), seq_heads(p['gla_log_a'], GLA_KEY_DIM),
                               caches['state_gla'][l])
    moba_q = seq_heads(p['moba_q'], MOBA_HEAD_DIM).reshape(DEC_BATCH, Q_ROWS, MOBA_HEAD_DIM)
    moba_o = _moba_sample(page_table, moba_q, p['moba_k'].reshape(DEC_BATCH, DEC_SEQ, MOBA_HEAD_DIM),
                          p['moba_v'].reshape(DEC_BATCH, DEC_SEQ, MOBA_HEAD_DIM),
                          caches['cache_moba_k'].reshape(DEPTH, -1, PAGE_SIZE, MOBA_HEAD_DIM),
                          caches['cache_moba_v'].reshape(DEPTH, -1, PAGE_SIZE, MOBA_HEAD_DIM), l)
    moba_o = back(moba_o.reshape(DEC_BATCH, N_HEADS, DEC_SEQ, MOBA_HEAD_DIM), MOBA_HEAD_DIM)

    to_rows = lambda t, d: jnp.swapaxes(t.reshape(N_HEADS, DEC_BATCH, DEC_SEQ, d), 0, 1).reshape(DEC_BATCH, Q_ROWS, d)
    mla_lat = _mla_sample(page_table, to_rows(p['mla_q_lat'], MLA_KV_RANK), to_rows(p['mla_q_rope'], MLA_ROPE_DIM),
                          p['mla_c'].reshape(DEC_BATCH, DEC_SEQ, MLA_KV_RANK),
                          p['mla_kr'].reshape(DEC_BATCH, DEC_SEQ, MLA_ROPE_DIM),
                          caches['cache_mla_latent'], jnp.swapaxes(caches['cache_mla_rope'], 2, 3), l)
    mla_lat = jnp.swapaxes(mla_lat.reshape(DEC_BATCH, N_HEADS, DEC_SEQ, MLA_KV_RANK), 0, 1).reshape(N_HEADS, m, MLA_KV_RANK)
    mla_o = _matmul_heads(mla_lat.astype(BF16), LW['w_uv'], 512, BF16)
    mla_o = jnp.swapaxes(mla_o, 0, 1).reshape(m, BRANCH_WIDTH)

    x, x_bf = _finish_layer(x, x_bf, p, l, W, LW, back(ret_o, RET_VAL_DIM), moba_o, mla_o, back(gla_o, GLA_VAL_DIM), 512)
    new = (p['moba_k'].reshape(DEC_BATCH, DEC_SEQ, 1, MOBA_HEAD_DIM), p['moba_v'].reshape(DEC_BATCH, DEC_SEQ, 1, MOBA_HEAD_DIM),
           p['mla_c'].reshape(DEC_BATCH, DEC_SEQ, MLA_KV_RANK), p['mla_kr'].reshape(DEC_BATCH, DEC_SEQ, MLA_ROPE_DIM),
           ret_s, gla_s)
    return x, x_bf, new


def kernel(x_prompt, x_sample, cache_moba_k, cache_moba_v, cache_mla_latent, cache_mla_rope, state_retention, state_gla, page_table, w_in, b_gate, ret_norm_g, mla_q_norm_g, mla_w_uq, mla_kv_norm_g, mla_w_uk, mla_w_uv, gla_w_a, gla_b_a, gla_norm_g, w_branch, w_out, ln1_g, ln1_b, ffn_w_gate, ffn_w_up, ffn_w_down, ln2_g, ln2_b):
    W = dict(w_in=w_in, b_gate=b_gate, ret_norm_g=ret_norm_g, mla_q_norm_g=mla_q_norm_g, mla_w_uq=mla_w_uq,
             mla_kv_norm_g=mla_kv_norm_g, mla_w_uk=mla_w_uk, mla_w_uv=mla_w_uv, gla_w_a=gla_w_a, gla_b_a=gla_b_a,
             gla_norm_g=gla_norm_g, w_branch=w_branch, w_out=w_out, ln1_g=ln1_g, ln1_b=ln1_b,
             ffn_w_gate=ffn_w_gate, ffn_w_up=ffn_w_up, ffn_w_down=ffn_w_down, ln2_g=ln2_g, ln2_b=ln2_b)
    caches = dict(cache_moba_k=cache_moba_k, cache_moba_v=cache_moba_v, cache_mla_latent=cache_mla_latent,
                  cache_mla_rope=cache_mla_rope, state_retention=state_retention, state_gla=state_gla)
    xp = x_prompt.reshape(BATCH * SEQ, D_MODEL)
    xs = x_sample.reshape(DEC_BATCH * DEC_SEQ, D_MODEL)
    xp_bf, xs_bf = xp.astype(BF16), xs.astype(BF16)
    new_p, new_s = [], []
    for l in range(DEPTH):
        LW = _layer_weights(l, W)
        xp, xp_bf, st_p = _prompt_layer(xp, xp_bf, l, W, LW)
        xs, xs_bf, st_s = _sample_layer(xs, xs_bf, l, W, LW, caches, page_table)
        new_p.append(st_p)
        new_s.append(st_s)
    stack = lambda states, i: jnp.stack([s[i] for s in states], axis=0)
    return ((xp.reshape(BATCH, SEQ, D_MODEL), xs.reshape(DEC_BATCH, DEC_SEQ, D_MODEL))
            + tuple(stack(new_p, i) for i in range(6)) + tuple(stack(new_s, i) for i in range(6)))
```

```python
import functools

import numpy as np
import jax
import jax.numpy as jnp
from jax import lax
from jax.experimental import pallas as pl
from jax.experimental.pallas import tpu as pltpu

D_MODEL = 2048
BATCH = 4
SEQ = 2048
DEPTH = 2
DEC_BATCH = 128
DEC_SEQ = 4
PAST_LEN = 16384
PAGE_SIZE = 128
N_PAGES = PAST_LEN // PAGE_SIZE

RET_HEADS = 4
RET_KEY_DIM = 64
RET_VAL_DIM = 128
MOBA_HEADS = 4
MOBA_HEAD_DIM = 128
MOBA_BLOCK = 256
MOBA_TOPK = 3
MLA_HEADS = 4
MLA_Q_RANK = 384
MLA_KV_RANK = 256
MLA_NOPE_DIM = 128
MLA_ROPE_DIM = 64
MLA_V_DIM = 128
GLA_HEADS = 4
GLA_KEY_DIM = 64
GLA_VAL_DIM = 128
GLA_GATE_RANK = 16
GLA_GATE_TAU = 16.0
N_BRANCH = 4
BRANCH_WIDTH = 512
FFN_HIDDEN = 5632
ROPE_THETA = 10000.0
DN_ALPHA = (2 * DEPTH) ** 0.25
LN_EPS = 1e-5
RMS_EPS = 1e-6
F32 = jnp.float32
BF16 = jnp.bfloat16

N_HEADS = 4
MAIN_COLS = 5120
SLAB_GA = 4096
SLAB_GR = 4608
GATE_COL0 = 4560
MLA_SCALE = (MLA_NOPE_DIM + MLA_ROPE_DIM) ** -0.5
MOBA_SCALE = MOBA_HEAD_DIM ** -0.5

VMEM_LIMIT = 56 * 1024 * 1024
NEG_INF = float("-inf")


def _cparams(n_axes):
    return pltpu.CompilerParams(dimension_semantics=("arbitrary",) * n_axes, vmem_limit_bytes=VMEM_LIMIT)


def _dot(a, b):
    return jnp.dot(a, b, preferred_element_type=F32)


def _dot_nt(a, b):
    return lax.dot_general(a, b, (((1,), (1,)), ((), ())), preferred_element_type=F32)


def _dot_tn(a, b):
    return lax.dot_general(a, b, (((0,), (0,)), ((), ())), preferred_element_type=F32)


def _sigmoid(x):
    return 1.0 / (1.0 + jnp.exp(-x))


def _mm_kernel(a_ref, b_ref, o_ref):
    o_ref[...] = _dot(a_ref[...], b_ref[...]).astype(o_ref.dtype)


def _matmul(a, b, tm, tn, out_dtype):
    m, k = a.shape
    n = b.shape[1]
    return pl.pallas_call(
        _mm_kernel, grid=(m // tm, n // tn),
        in_specs=[pl.BlockSpec((tm, k), lambda i, j: (i, 0)), pl.BlockSpec((k, tn), lambda i, j: (0, j))],
        out_specs=pl.BlockSpec((tm, tn), lambda i, j: (i, j)),
        out_shape=jax.ShapeDtypeStruct((m, n), out_dtype),
        compiler_params=_cparams(2), name="mm")(a, b)


def _proj_kernel(x_ref, w_ref, o_ref, w_bf):
    @pl.when(pl.program_id(1) == 0)
    def _():
        w_bf[...] = w_ref[0].astype(BF16)

    o_ref[...] = _dot_nt(x_ref[...], w_bf[...])


def _slab_col_start(j):
    return jnp.where(j < 6, j * 512, jnp.where(j < 9, 3008 + (j - 6) * 512, 4048))


def _proj(x, w_in_t, layer, tm):
    m = x.shape[0]
    return pl.pallas_call(
        _proj_kernel, grid=(MAIN_COLS // 512, m // tm),
        in_specs=[pl.BlockSpec((tm, D_MODEL), lambda j, i: (i, 0)),
                  pl.BlockSpec((pl.Element(1), pl.Element(512), pl.Element(D_MODEL)),
                               lambda j, i: (layer, pl.multiple_of(_slab_col_start(j), 8), 0))],
        out_specs=pl.BlockSpec((tm, 512), lambda j, i: (i, j)),
        out_shape=jax.ShapeDtypeStruct((m, MAIN_COLS), F32),
        scratch_shapes=[pltpu.VMEM((512, D_MODEL), BF16)],
        compiler_params=_cparams(2), name="proj")(x, w_in_t)


def _mm_heads_kernel(a_ref, b_ref, o_ref):
    o_ref[0] = _dot(a_ref[0], b_ref[0]).astype(o_ref.dtype)


def _matmul_heads(a, b, tm, out_dtype):
    h, m, k = a.shape
    n = b.shape[2]
    return pl.pallas_call(
        _mm_heads_kernel, grid=(h, m // tm),
        in_specs=[pl.BlockSpec((1, tm, k), lambda g, i: (g, i, 0)), pl.BlockSpec((1, k, n), lambda g, i: (g, 0, 0))],
        out_specs=pl.BlockSpec((1, tm, n), lambda g, i: (g, i, 0)),
        out_shape=jax.ShapeDtypeStruct((h, m, n), out_dtype),
        compiler_params=_cparams(2), name="mm_heads")(a, b)


def _ffn_up_kernel(x_ref, wg_ref, wu_ref, o_ref, wg_bf, wu_bf):
    @pl.when(pl.program_id(1) == 0)
    def _():
        wg_bf[...] = wg_ref[...].astype(BF16)
        wu_bf[...] = wu_ref[...].astype(BF16)

    x = x_ref[...]
    g = _dot(x, wg_bf[...])
    u = _dot(x, wu_bf[...])
    o_ref[...] = (g * _sigmoid(g) * u).astype(o_ref.dtype)


def _ffn_up(x, wg, wu, layer, tm, tn):
    m, k = x.shape
    n = wg.shape[2]
    w_spec = pl.BlockSpec((None, k, tn), lambda j, i: (layer, 0, j))
    return pl.pallas_call(
        _ffn_up_kernel, grid=(n // tn, m // tm),
        in_specs=[pl.BlockSpec((tm, k), lambda j, i: (i, 0)), w_spec, w_spec],
        out_specs=pl.BlockSpec((tm, tn), lambda j, i: (i, j)),
        out_shape=jax.ShapeDtypeStruct((m, n), BF16),
        scratch_shapes=[pltpu.VMEM((k, tn), BF16), pltpu.VMEM((k, tn), BF16)],
        compiler_params=_cparams(2), name="ffn_up")(x, wg, wu)


def _mm_res_ln_kernel(a_ref, b_ref, res_ref, g_ref, beta_ref, o_ref, obf_ref, acc_ref):
    k = pl.program_id(1)

    @pl.when(k == 0)
    def _():
        acc_ref[...] = jnp.zeros_like(acc_ref)

    acc_ref[...] += _dot(a_ref[...], b_ref[...].astype(BF16))

    @pl.when(k == pl.num_programs(1) - 1)
    def _():
        y = DN_ALPHA * res_ref[...] + acc_ref[...]
        mu = jnp.mean(y, axis=-1, keepdims=True)
        yc = y - mu
        var = jnp.mean(yc * yc, axis=-1, keepdims=True)
        out = yc * lax.rsqrt(var + LN_EPS) * g_ref[...] + beta_ref[...]
        o_ref[...] = out
        obf_ref[...] = out.astype(BF16)


def _mm_res_ln(a, b, layer, res, g, beta, tm, tk):
    m, k = a.shape
    n = b.shape[2]
    return pl.pallas_call(
        _mm_res_ln_kernel, grid=(m // tm, k // tk),
        in_specs=[pl.BlockSpec((tm, tk), lambda i, j: (i, j)), pl.BlockSpec((None, tk, n), lambda i, j: (layer, j, 0)),
                  pl.BlockSpec((tm, n), lambda i, j: (i, 0)), pl.BlockSpec((1, n), lambda i, j: (0, 0)),
                  pl.BlockSpec((1, n), lambda i, j: (0, 0))],
        out_specs=[pl.BlockSpec((tm, n), lambda i, j: (i, 0)), pl.BlockSpec((tm, n), lambda i, j: (i, 0))],
        out_shape=[jax.ShapeDtypeStruct((m, n), F32), jax.ShapeDtypeStruct((m, n), BF16)],
        scratch_shapes=[pltpu.VMEM((tm, n), F32)],
        compiler_params=_cparams(2), name="mm_res_ln")(a, b, res, g.reshape(1, n), beta.reshape(1, n))


def _merge_kernel(x_ref, wg_ref, bg_ref, br_ref, wb_ref, o_ref, acc_ref):
    n = pl.program_id(2)
    gate = _sigmoid(_dot_nt(x_ref[...], wg_ref[0].astype(BF16)) + bg_ref[...])
    contrib = gate * _dot(br_ref[0], wb_ref[...].astype(BF16))

    @pl.when(n == 0)
    def _():
        acc_ref[...] = contrib

    @pl.when(n > 0)
    def _():
        acc_ref[...] += contrib

    @pl.when(n == N_BRANCH - 1)
    def _():
        o_ref[...] = acc_ref[...].astype(o_ref.dtype)


def _merge(x, w_in_t, b_gate, branches, w_branch, layer, tm, td):
    m = x.shape[0]
    nd = D_MODEL // td
    return pl.pallas_call(
        _merge_kernel, grid=(m // tm, nd, N_BRANCH),
        in_specs=[pl.BlockSpec((tm, D_MODEL), lambda i, d, n: (i, 0)),
                  pl.BlockSpec((pl.Element(1), pl.Element(td), pl.Element(D_MODEL)),
                               lambda i, d, n: (layer, pl.multiple_of(GATE_COL0 + (n * nd + d) * td, 8), 0)),
                  pl.BlockSpec((1, td), lambda i, d, n: (0, n * nd + d)),
                  pl.BlockSpec((1, tm, BRANCH_WIDTH), lambda i, d, n: (n, i, 0)),
                  pl.BlockSpec((None, None, BRANCH_WIDTH, td), lambda i, d, n: (layer, n, 0, d))],
        out_specs=pl.BlockSpec((tm, td), lambda i, d, n: (i, d)),
        out_shape=jax.ShapeDtypeStruct((m, D_MODEL), BF16),
        scratch_shapes=[pltpu.VMEM((tm, td), F32)],
        compiler_params=_cparams(3), name="merge")(x, w_in_t, b_gate, branches, w_branch)


def _softmax_step(s, m_ref, l_ref, acc_ref, v_bf):
    m_prev = m_ref[...]
    m_new = jnp.maximum(m_prev, jnp.max(s, axis=-1, keepdims=True))
    alpha = jnp.exp(m_prev - m_new)
    p = jnp.exp(s - m_new)
    l_ref[...] = alpha * l_ref[...] + jnp.sum(p, axis=-1, keepdims=True)
    acc_ref[...] = alpha * acc_ref[...] + _dot(p.astype(BF16), v_bf)
    m_ref[...] = m_new


def _mla_prompt_kernel(ql_ref, qr_ref, c_ref, kr_ref, wuv_ref, o_ref, m_ref, l_ref, acc_ref, *, tq, tk):
    qi = pl.program_id(1)
    kj = pl.program_id(2)
    rows = N_HEADS * tq

    @pl.when(kj == 0)
    def _():
        m_ref[...] = jnp.full_like(m_ref, NEG_INF)
        l_ref[...] = jnp.zeros_like(l_ref)
        acc_ref[...] = jnp.zeros_like(acc_ref)

    @pl.when(kj * tk <= qi * tq + tq - 1)
    def _():
        ql = ql_ref[...].reshape(rows, MLA_KV_RANK)
        qr = qr_ref[...].reshape(rows, MLA_ROPE_DIM)
        c = c_ref[...]
        s = (_dot_nt(ql, c) + _dot_nt(qr, kr_ref[...])) * MLA_SCALE
        qpos = qi * tq + (lax.broadcasted_iota(jnp.int32, (rows, tk), 0) & (tq - 1))
        kpos = kj * tk + lax.broadcasted_iota(jnp.int32, (rows, tk), 1)
        s = jnp.where(kpos <= qpos, s, NEG_INF)
        _softmax_step(s, m_ref, l_ref, acc_ref, c)

    @pl.when(kj == pl.num_programs(2) - 1)
    def _():
        o = (acc_ref[...] / l_ref[...]).astype(BF16)
        for h in range(N_HEADS):
            o_ref[:, h * MLA_V_DIM:(h + 1) * MLA_V_DIM] = _dot(o[h * tq:(h + 1) * tq], wuv_ref[h]).astype(o_ref.dtype)


def _mla_prompt(ql, qr, c, kr, wuv, tq=256, tk=512):
    nq = SEQ // tq
    nk = SEQ // tk

    def kv_map(b, i, j):
        return (b * nk + jnp.minimum(j, (i * tq + tq - 1) // tk), 0)

    return pl.pallas_call(
        functools.partial(_mla_prompt_kernel, tq=tq, tk=tk), grid=(BATCH, nq, nk),
        in_specs=[pl.BlockSpec((N_HEADS, tq, MLA_KV_RANK), lambda b, i, j: (0, b * nq + i, 0)),
                  pl.BlockSpec((N_HEADS, tq, MLA_ROPE_DIM), lambda b, i, j: (0, b * nq + i, 0)),
                  pl.BlockSpec((tk, MLA_KV_RANK), kv_map),
                  pl.BlockSpec((tk, MLA_ROPE_DIM), kv_map),
                  pl.BlockSpec((N_HEADS, MLA_KV_RANK, MLA_V_DIM), lambda b, i, j: (0, 0, 0))],
        out_specs=pl.BlockSpec((tq, BRANCH_WIDTH), lambda b, i, j: (b * nq + i, 0)),
        out_shape=jax.ShapeDtypeStruct((BATCH * SEQ, BRANCH_WIDTH), BF16),
        scratch_shapes=[pltpu.VMEM((N_HEADS * tq, 1), F32), pltpu.VMEM((N_HEADS * tq, 1), F32),
                        pltpu.VMEM((N_HEADS * tq, MLA_KV_RANK), F32)],
        compiler_params=_cparams(3), name="mla_prompt")(ql, qr, c, kr, wuv)


def _split_bf16(x):
    hi = x.astype(BF16)
    lo = (x - hi.astype(F32)).astype(BF16)
    return hi, lo


def _topk_select(gate, n_valid, n_cand):
    lane = lax.broadcasted_iota(jnp.int32, gate.shape, 1)
    gm = jnp.where(lane < n_valid, gate, NEG_INF)
    rank = jnp.zeros(gate.shape, jnp.int32)
    for a in range(n_cand):
        ga = gm[:, a:a + 1]
        beats = (ga > gm) | ((ga == gm) & (lane > a))
        rank = rank + beats.astype(jnp.int32)
    return (rank < MOBA_TOPK) & (lane < n_valid)


def _moba_prompt_kernel(q_ref, k_ref, v_ref, o_ref, m_ref, l_ref, acc_ref, *, tq):
    i = pl.program_id(1)
    rows = N_HEADS * tq
    nb = SEQ // MOBA_BLOCK
    qf = jnp.concatenate([q_ref[:, h * MOBA_HEAD_DIM:(h + 1) * MOBA_HEAD_DIM] for h in range(N_HEADS)], axis=0)
    qb = qf.astype(BF16)

    means = jnp.sum(k_ref[...].reshape(nb, MOBA_BLOCK, MOBA_HEAD_DIM), axis=1) * (1.0 / MOBA_BLOCK)
    means = jnp.concatenate([means, jnp.zeros((128 - nb, MOBA_HEAD_DIM), F32)], axis=0)
    q_hi, q_lo = _split_bf16(qf)
    mn_hi, mn_lo = _split_bf16(means)
    gate = _dot_nt(q_hi, mn_hi) + _dot_nt(q_hi, mn_lo) + _dot_nt(q_lo, mn_hi)
    sel = _topk_select(gate, i, nb).astype(BF16)
    blk_row = lax.broadcasted_iota(jnp.int32, (128, MOBA_BLOCK), 0)

    own0 = pl.multiple_of(i * MOBA_BLOCK, MOBA_BLOCK)
    s = _dot_nt(qb, k_ref[pl.ds(own0, MOBA_BLOCK), :].astype(BF16)) * MOBA_SCALE
    qrel = lax.broadcasted_iota(jnp.int32, (rows, MOBA_BLOCK), 0) & (tq - 1)
    krel = lax.broadcasted_iota(jnp.int32, (rows, MOBA_BLOCK), 1)
    s = jnp.where(krel <= qrel, s, NEG_INF)
    m_ref[...] = jnp.full_like(m_ref, NEG_INF)
    l_ref[...] = jnp.zeros_like(l_ref)
    acc_ref[...] = jnp.zeros_like(acc_ref)
    _softmax_step(s, m_ref, l_ref, acc_ref, v_ref[pl.ds(own0, MOBA_BLOCK), :].astype(BF16))

    def past_block(j, carry):
        j0 = pl.multiple_of(j * MOBA_BLOCK, MOBA_BLOCK)
        sj = _dot_nt(qb, k_ref[pl.ds(j0, MOBA_BLOCK), :].astype(BF16)) * MOBA_SCALE
        picked = _dot(sel, (blk_row == j).astype(BF16)) > 0.5
        sj = jnp.where(picked, sj, NEG_INF)
        _softmax_step(sj, m_ref, l_ref, acc_ref, v_ref[pl.ds(j0, MOBA_BLOCK), :].astype(BF16))
        return carry

    lax.fori_loop(0, i, past_block, 0)
    o = acc_ref[...] / l_ref[...]
    for h in range(N_HEADS):
        o_ref[:, h * MOBA_HEAD_DIM:(h + 1) * MOBA_HEAD_DIM] = o[h * tq:(h + 1) * tq].astype(o_ref.dtype)


def _moba_prompt(proj):
    tq = MOBA_BLOCK
    nq = SEQ // tq
    return pl.pallas_call(
        functools.partial(_moba_prompt_kernel, tq=tq), grid=(BATCH, nq),
        in_specs=[pl.BlockSpec((tq, 512), lambda b, i: (b * nq + i, 1536 // 512)),
                  pl.BlockSpec((SEQ, 128), lambda b, i: (b, 2048 // 128)),
                  pl.BlockSpec((SEQ, 128), lambda b, i: (b, 2176 // 128))],
        out_specs=pl.BlockSpec((tq, BRANCH_WIDTH), lambda b, i: (b * nq + i, 0)),
        out_shape=jax.ShapeDtypeStruct((BATCH * SEQ, BRANCH_WIDTH), BF16),
        scratch_shapes=[pltpu.VMEM((N_HEADS * tq, 1), F32), pltpu.VMEM((N_HEADS * tq, 1), F32),
                        pltpu.VMEM((N_HEADS * tq, MOBA_HEAD_DIM), F32)],
        compiler_params=_cparams(2), name="moba_prompt")(proj, proj, proj)


def _ret_log_gamma():
    return np.log1p(-np.exp2(-5.0 - np.arange(RET_HEADS, dtype=np.float32))).astype(np.float32)


def _ret_tables(chunk):
    log_g = jnp.asarray(_ret_log_gamma())
    idx = jnp.arange(chunk, dtype=F32)
    diff = idx[:, None] - idx[None, :]
    decay = jnp.where(diff >= 0, jnp.exp(jnp.maximum(diff, 0.0)[None] * log_g[:, None, None]), 0.0)
    q_dec = jnp.exp((idx + 1.0)[None, :] * log_g[:, None])[..., None]
    k_dec = jnp.exp((chunk - 1.0 - idx)[None, :] * log_g[:, None])[:, None, :]
    c_dec = [float(np.exp(np.float32(chunk) * g)) for g in _ret_log_gamma()]
    return decay, q_dec, k_dec, c_dec


def _rope_tables(pos, inv_freq):
    ang = pos.astype(F32)[:, None] * inv_freq[None, :]
    cos, sin = jnp.cos(ang), jnp.sin(ang)
    return jnp.tile(cos, (1, 4)), jnp.tile(jnp.concatenate([-sin, sin], axis=1), (1, 2))


def _rope128(x, cos, sin_signed):
    lane = lax.broadcasted_iota(jnp.int32, x.shape, 1)
    partner = jnp.where((lane & 63) < 32, pltpu.roll(x, 96, 1), pltpu.roll(x, 32, 1))
    return x * cos + partner * sin_signed


def _head_rms_gate(o, norm_g, gate):
    o = o * lax.rsqrt(jnp.mean(o * o, axis=-1, keepdims=True) + RMS_EPS) * norm_g
    return o * (gate * _sigmoid(gate))


def _ret_prompt_kernel(qk_ref, v_ref, g_ref, cos_ref, sin_ref, dec_ref, qd_ref, kd_ref, ng_ref, o_ref, s_ref, st_ref,
                       *, c_dec):
    c = pl.program_id(1)

    @pl.when(c == 0)
    def _():
        st_ref[...] = jnp.zeros_like(st_ref)

    cos, sin = cos_ref[...], sin_ref[...]
    roped = [_rope128(qk_ref[:, j * 128:(j + 1) * 128], cos, sin) for j in range(4)]
    for h in range(RET_HEADS):
        lo = (h % 2) * RET_KEY_DIM
        q = roped[h // 2][:, lo:lo + RET_KEY_DIM]
        k = roped[2 + h // 2][:, lo:lo + RET_KEY_DIM] * (RET_KEY_DIM ** -0.5)
        v = v_ref[:, h * RET_VAL_DIM:(h + 1) * RET_VAL_DIM].astype(BF16)
        state = st_ref[h]
        att = _dot_nt(q.astype(BF16), k.astype(BF16)) * dec_ref[h]
        o = _dot(att.astype(BF16), v) + _dot((q * qd_ref[h]).astype(BF16), state.astype(BF16))
        st_ref[h] = state * c_dec[h] + _dot_tn((k * kd_ref[h]).astype(BF16), v)
        cols = slice(h * RET_VAL_DIM, (h + 1) * RET_VAL_DIM)
        o_ref[:, cols] = _head_rms_gate(o, ng_ref[:, cols], g_ref[:, cols]).astype(o_ref.dtype)

    @pl.when(c == pl.num_programs(1) - 1)
    def _():
        s_ref[0] = st_ref[...]


def _ret_prompt(proj, norm_g, chunk=256):
    nc = SEQ // chunk
    decay, q_dec, k_dec, c_dec = _ret_tables(chunk)
    ret_inv = 1.0 / (ROPE_THETA ** jnp.linspace(0.0, 1.0, RET_KEY_DIM // 2, dtype=F32))
    cos, sin = _rope_tables(jnp.arange(SEQ, dtype=jnp.int32), ret_inv)
    col_blk = lambda j: pl.BlockSpec((chunk, 512), lambda b, c: (b * nc + c, j))
    const = lambda shape: pl.BlockSpec(shape, lambda b, c: (0,) * len(shape))
    return pl.pallas_call(
        functools.partial(_ret_prompt_kernel, c_dec=c_dec), grid=(BATCH, nc),
        in_specs=[col_blk(0), col_blk(1), col_blk(2),
                  pl.BlockSpec((chunk, 128), lambda b, c: (c, 0)), pl.BlockSpec((chunk, 128), lambda b, c: (c, 0)),
                  const((RET_HEADS, chunk, chunk)), const((RET_HEADS, chunk, 1)), const((RET_HEADS, chunk, 1)),
                  const((1, BRANCH_WIDTH))],
        out_specs=[pl.BlockSpec((chunk, BRANCH_WIDTH), lambda b, c: (b * nc + c, 0)),
                   pl.BlockSpec((1, RET_HEADS, RET_KEY_DIM, RET_VAL_DIM), lambda b, c: (b, 0, 0, 0))],
        out_shape=[jax.ShapeDtypeStruct((BATCH * SEQ, BRANCH_WIDTH), BF16),
                   jax.ShapeDtypeStruct((BATCH, RET_HEADS, RET_KEY_DIM, RET_VAL_DIM), F32)],
        scratch_shapes=[pltpu.VMEM((RET_HEADS, RET_KEY_DIM, RET_VAL_DIM), F32)],
        compiler_params=_cparams(2), name="ret_prompt")(
            proj, proj, proj, cos, sin, decay, q_dec, jnp.swapaxes(k_dec, 1, 2), norm_g.reshape(1, BRANCH_WIDTH))


GLA_C = 64
GLA_SUB = 16


def _gla_chunk(q, k, v, la, state_t):
    c = GLA_C
    row = lax.broadcasted_iota(jnp.int32, (c, c), 0)
    col = lax.broadcasted_iota(jnp.int32, (c, c), 1)
    tril = (col <= row).astype(BF16)
    la_hi = la.astype(BF16)
    r1 = la - la_hi.astype(F32)
    la_mid = r1.astype(BF16)
    la_lo = (r1 - la_mid.astype(F32)).astype(BF16)
    b = _dot(tril, la_hi) + _dot(tril, la_mid) + _dot(tril, la_lo)
    b_last = b[c - 1:c, :]
    v_bf = v.astype(BF16)
    o = _dot_nt((q * jnp.exp(b)).astype(BF16), state_t.astype(BF16))
    k_state = (k * jnp.exp(b_last - b)).astype(BF16)
    new_state_t = state_t * jnp.exp(b_last) + _dot_tn(v_bf, k_state)

    rsub = lax.broadcasted_iota(jnp.int32, (c, GLA_KEY_DIM), 0)
    tri3 = (lax.broadcasted_iota(jnp.int32, (GLA_SUB, GLA_SUB, GLA_KEY_DIM), 1)
            <= lax.broadcasted_iota(jnp.int32, (GLA_SUB, GLA_SUB, GLA_KEY_DIM), 0))
    o_rows = []
    for blk in range(c // GLA_SUB):
        r0 = blk * GLA_SUB
        qi, ki, bi = q[r0:r0 + GLA_SUB], k[r0:r0 + GLA_SUB], b[r0:r0 + GLA_SUB]
        rel = jnp.where(tri3, jnp.exp(jnp.minimum(bi[:, None, :] - bi[None, :, :], 0.0)), 0.0)
        att_d = jnp.sum(qi[:, None, :] * ki[None, :, :] * rel, axis=-1)
        o_blk = o[r0:r0 + GLA_SUB] + _dot(att_d.astype(BF16), v_bf[r0:r0 + GLA_SUB])
        if blk > 0:
            ref_b = b[r0 - 1:r0, :]
            q_off = (qi * jnp.exp(bi - ref_b)).astype(BF16)
            k_off = jnp.where(rsub < r0, k * jnp.exp(jnp.minimum(ref_b - b, 0.0)), 0.0).astype(BF16)
            o_blk = o_blk + _dot(_dot_nt(q_off, k_off).astype(BF16), v_bf)
        o_rows.append(o_blk)
    return jnp.concatenate(o_rows, axis=0), new_state_t


def _log_sigmoid(x):
    return jnp.minimum(x, 0.0) - jnp.log1p(jnp.exp(-jnp.abs(x)))


def _gla_prompt_kernel(qk_ref, v_ref, ga_ref, r_ref, wa_ref, ba_ref, ng_ref, o_ref, s_ref, st_ref, la_ref, *, n_chunks):
    step = pl.program_id(1)

    @pl.when(step == 0)
    def _():
        st_ref[...] = jnp.zeros_like(st_ref)

    la_ref[...] = _log_sigmoid(_dot(ga_ref[...].astype(BF16), wa_ref[...]) + ba_ref[...]) * (1.0 / GLA_GATE_TAU)

    def chunk_body(ci, carry):
        r0 = pl.multiple_of(ci * GLA_C, GLA_C)
        rows = pl.ds(r0, GLA_C)
        for h in range(GLA_HEADS):
            kcols = slice(h * GLA_KEY_DIM, (h + 1) * GLA_KEY_DIM)
            vcols = slice(h * GLA_VAL_DIM, (h + 1) * GLA_VAL_DIM)
            q = qk_ref[rows, kcols] * (GLA_KEY_DIM ** -0.5)
            k = qk_ref[rows, GLA_HEADS * GLA_KEY_DIM + h * GLA_KEY_DIM:GLA_HEADS * GLA_KEY_DIM + (h + 1) * GLA_KEY_DIM]
            o, new_state = _gla_chunk(q, k, v_ref[rows, vcols], la_ref[rows, kcols], st_ref[h])
            st_ref[h] = new_state
            o_ref[rows, vcols] = _head_rms_gate(o, ng_ref[:, vcols], r_ref[rows, vcols]).astype(o_ref.dtype)
        return carry

    lax.fori_loop(0, n_chunks, chunk_body, 0)

    @pl.when(step == pl.num_programs(1) - 1)
    def _():
        s_ref[0] = st_ref[...]


def _gla_prompt(proj, w_a, b_a, norm_g, tile=256):
    nt = SEQ // tile
    width = GLA_HEADS * GLA_KEY_DIM
    w_pad = jnp.zeros((128, width), F32).at[:GLA_GATE_RANK].set(w_a.reshape(GLA_GATE_RANK, width)).astype(BF16)
    col_blk = lambda w, j: pl.BlockSpec((tile, w), lambda b, c: (b * nt + c, j))
    const = lambda shape: pl.BlockSpec(shape, lambda b, c: (0,) * len(shape))
    return pl.pallas_call(
        functools.partial(_gla_prompt_kernel, n_chunks=tile // GLA_C), grid=(BATCH, nt),
        in_specs=[col_blk(512, 3072 // 512), col_blk(512, 3584 // 512), col_blk(128, SLAB_GA // 128), col_blk(512, SLAB_GR // 512),
                  const((128, width)), const((1, width)), const((1, BRANCH_WIDTH))],
        out_specs=[pl.BlockSpec((tile, BRANCH_WIDTH), lambda b, c: (b * nt + c, 0)),
                   pl.BlockSpec((1, GLA_HEADS, GLA_VAL_DIM, GLA_KEY_DIM), lambda b, c: (b, 0, 0, 0))],
        out_shape=[jax.ShapeDtypeStruct((BATCH * SEQ, BRANCH_WIDTH), BF16),
                   jax.ShapeDtypeStruct((BATCH, GLA_HEADS, GLA_VAL_DIM, GLA_KEY_DIM), F32)],
        scratch_shapes=[pltpu.VMEM((GLA_HEADS, GLA_VAL_DIM, GLA_KEY_DIM), F32), pltpu.VMEM((tile, width), F32)],
        compiler_params=_cparams(2), name="gla_prompt")(
            proj, proj, proj, proj, w_pad, b_a.reshape(1, width), norm_g.reshape(1, BRANCH_WIDTH))


SAMPLE_NB = 8


def _ret_sample_kernel(q_ref, kt_ref, v_ref, s0_ref, dec_ref, qd_ref, kd_ref, cd_ref, o_ref, s_ref):
    nb = SAMPLE_NB * RET_HEADS
    q = q_ref[...].reshape(nb, DEC_SEQ, RET_KEY_DIM)
    kt = kt_ref[...].reshape(nb, RET_KEY_DIM, DEC_SEQ)
    v = v_ref[...].reshape(nb, DEC_SEQ, RET_VAL_DIM).astype(BF16)
    s0 = s0_ref[...].reshape(nb, RET_KEY_DIM, RET_VAL_DIM)
    att = jnp.einsum('bqd,bdk->bqk', q.astype(BF16), kt.astype(BF16), preferred_element_type=F32) * dec_ref[...]
    o = (jnp.einsum('bqk,bkv->bqv', att.astype(BF16), v, preferred_element_type=F32)
         + jnp.einsum('bqd,bdv->bqv', (q * qd_ref[...]).astype(BF16), s0.astype(BF16), preferred_element_type=F32))
    s_new = s0 * cd_ref[...] + jnp.einsum('bdt,btv->bdv', (kt * kd_ref[...]).astype(BF16), v, preferred_element_type=F32)
    o_ref[...] = o.reshape(SAMPLE_NB, RET_HEADS, DEC_SEQ, RET_VAL_DIM)
    s_ref[...] = s_new.reshape(SAMPLE_NB, RET_HEADS, RET_KEY_DIM, RET_VAL_DIM)


def _ret_sample(q, kt, v, s0):
    decay, q_dec, k_dec, c_dec = _ret_tables(DEC_SEQ)
    rep = lambda t: jnp.tile(t, (SAMPLE_NB,) + (1,) * (t.ndim - 1))
    cd = rep(jnp.asarray(c_dec, F32).reshape(RET_HEADS, 1, 1))
    blk = lambda *dims: pl.BlockSpec((SAMPLE_NB, RET_HEADS) + dims, lambda i: (i, 0, 0, 0))
    full = lambda t: pl.BlockSpec(t.shape, lambda i: (0, 0, 0))
    tabs = [rep(decay), rep(q_dec), rep(k_dec), cd]
    return pl.pallas_call(
        _ret_sample_kernel, grid=(DEC_BATCH // SAMPLE_NB,),
        in_specs=[blk(DEC_SEQ, RET_KEY_DIM), blk(RET_KEY_DIM, DEC_SEQ), blk(DEC_SEQ, RET_VAL_DIM),
                  blk(RET_KEY_DIM, RET_VAL_DIM)] + [full(t) for t in tabs],
        out_specs=[blk(DEC_SEQ, RET_VAL_DIM), blk(RET_KEY_DIM, RET_VAL_DIM)],
        out_shape=[jax.ShapeDtypeStruct((DEC_BATCH, RET_HEADS, DEC_SEQ, RET_VAL_DIM), F32),
                   jax.ShapeDtypeStruct((DEC_BATCH, RET_HEADS, RET_KEY_DIM, RET_VAL_DIM), F32)],
        compiler_params=_cparams(1), name="ret_sample")(q, kt, v, s0, *tabs)


def _gla_sample_kernel(q_ref, k_ref, kt_ref, v_ref, la_ref, lat_ref, s0_ref, o_ref, s_ref):
    nb = SAMPLE_NB * GLA_HEADS
    t_n = DEC_SEQ
    q = q_ref[...].reshape(nb, t_n, GLA_KEY_DIM)
    k = k_ref[...].reshape(nb, t_n, GLA_KEY_DIM)
    v = v_ref[...].reshape(nb, t_n, GLA_VAL_DIM)
    la = la_ref[...].reshape(nb, t_n, GLA_KEY_DIM)
    kt = kt_ref[...].reshape(nb, GLA_KEY_DIM, t_n)
    lat = lat_ref[...].reshape(nb, GLA_KEY_DIM, t_n)
    s0 = s0_ref[...].reshape(nb, GLA_KEY_DIM, GLA_VAL_DIM)
    b_rows, bt_cols = [], []
    run, run_t = None, None
    for t in range(t_n):
        run = la[:, t, :] if run is None else run + la[:, t, :]
        run_t = lat[:, :, t:t + 1] if run_t is None else run_t + lat[:, :, t:t + 1]
        b_rows.append(run)
        bt_cols.append(run_t)
    b_last_t = bt_cols[-1]
    q_state = jnp.stack([q[:, t, :] * jnp.exp(b_rows[t]) for t in range(t_n)], axis=1)
    o = jnp.einsum('bqd,bdv->bqv', q_state.astype(BF16), s0.astype(BF16), preferred_element_type=F32)
    o_rows = []
    for i in range(t_n):
        o_i = o[:, i, :]
        for j in range(i + 1):
            w = jnp.sum(q[:, i, :] * k[:, j, :] * jnp.exp(jnp.minimum(b_rows[i] - b_rows[j], 0.0)),
                        axis=-1, keepdims=True)
            o_i = o_i + w * v[:, j, :]
        o_rows.append(o_i)
    k_state_t = jnp.concatenate([kt[:, :, t:t + 1] * jnp.exp(b_last_t - bt_cols[t]) for t in range(t_n)], axis=2)
    s_new = s0 * jnp.exp(b_last_t) + jnp.einsum('bdt,btv->bdv', k_state_t.astype(BF16), v.astype(BF16),
                                                  preferred_element_type=F32)
    o_ref[...] = jnp.stack(o_rows, axis=1).reshape(SAMPLE_NB, GLA_HEADS, t_n, GLA_VAL_DIM)
    s_ref[...] = s_new.reshape(SAMPLE_NB, GLA_HEADS, GLA_KEY_DIM, GLA_VAL_DIM)


def _gla_sample(q, k, v, la, s0):
    kt = jnp.swapaxes(k, 2, 3)
    lat = jnp.swapaxes(la, 2, 3)
    blk = lambda *dims: pl.BlockSpec((SAMPLE_NB, GLA_HEADS) + dims, lambda i: (i, 0, 0, 0))
    return pl.pallas_call(
        _gla_sample_kernel, grid=(DEC_BATCH // SAMPLE_NB,),
        in_specs=[blk(DEC_SEQ, GLA_KEY_DIM), blk(DEC_SEQ, GLA_KEY_DIM), blk(GLA_KEY_DIM, DEC_SEQ),
                  blk(DEC_SEQ, GLA_VAL_DIM), blk(DEC_SEQ, GLA_KEY_DIM), blk(GLA_KEY_DIM, DEC_SEQ),
                  blk(GLA_KEY_DIM, GLA_VAL_DIM)],
        out_specs=[blk(DEC_SEQ, GLA_VAL_DIM), blk(GLA_KEY_DIM, GLA_VAL_DIM)],
        out_shape=[jax.ShapeDtypeStruct((DEC_BATCH, GLA_HEADS, DEC_SEQ, GLA_VAL_DIM), F32),
                   jax.ShapeDtypeStruct((DEC_BATCH, GLA_HEADS, GLA_KEY_DIM, GLA_VAL_DIM), F32)],
        compiler_params=_cparams(1), name="gla_sample")(q, k, kt, v, la, lat, s0)


Q_ROWS = N_HEADS * DEC_SEQ
MLA_PAGES = 64
MLA_KEYS = MLA_PAGES * PAGE_SIZE
MLA_STEPS = N_PAGES // MLA_PAGES


def _new_token_mask():
    tok = lax.broadcasted_iota(jnp.int32, (Q_ROWS, DEC_SEQ), 0) & (DEC_SEQ - 1)
    key = lax.broadcasted_iota(jnp.int32, (Q_ROWS, DEC_SEQ), 1)
    return key <= tok


def _mla_sample_kernel(pt_ref, ql_ref, qr_ref, cn_ref, kn_ref, lat_hbm, rope_hbm, o_ref,
                       lat_buf, rope_buf, lat_sem, rope_sem, m_ref, l_ref, acc_ref, *, layer):
    s = pl.program_id(0)
    c = pl.program_id(1)
    g = s * MLA_STEPS + c
    slot = lax.rem(g, 2)

    def copies(seq, chunk, slot_):
        out = []
        for i in range(MLA_PAGES):
            page = pt_ref[seq, chunk * MLA_PAGES + i]
            out.append(pltpu.make_async_copy(lat_hbm.at[layer, page], lat_buf.at[slot_, pl.ds(i * PAGE_SIZE, PAGE_SIZE)],
                                             lat_sem.at[slot_]))
            out.append(pltpu.make_async_copy(rope_hbm.at[layer, page], rope_buf.at[slot_, i], rope_sem.at[slot_]))
        return out

    @pl.when(g == 0)
    def _():
        for cp in copies(0, 0, 0):
            cp.start()

    @pl.when(g + 1 < DEC_BATCH * MLA_STEPS)
    def _():
        nxt = g + 1
        for cp in copies(nxt // MLA_STEPS, lax.rem(nxt, MLA_STEPS), 1 - slot):
            cp.start()

    for cp in copies(s, c, slot):
        cp.wait()

    @pl.when(c == 0)
    def _():
        m_ref[...] = jnp.full_like(m_ref, NEG_INF)
        l_ref[...] = jnp.zeros_like(l_ref)
        acc_ref[...] = jnp.zeros_like(acc_ref)

    ql = ql_ref[0].astype(BF16)
    qr = qr_ref[0].astype(BF16)
    lat = lat_buf[slot].astype(BF16)
    s_rope = jnp.concatenate([_dot(qr, rope_buf[slot, i].astype(BF16)) for i in range(MLA_PAGES)], axis=1)
    sc = (_dot_nt(ql, lat) + s_rope) * MLA_SCALE
    _softmax_step(sc, m_ref, l_ref, acc_ref, lat)

    @pl.when(c == MLA_STEPS - 1)
    def _():
        cn = cn_ref[0].astype(BF16)
        sn = (_dot_nt(ql, cn) + _dot_nt(qr, kn_ref[0].astype(BF16))) * MLA_SCALE
        sn = jnp.where(_new_token_mask(), sn, NEG_INF)
        _softmax_step(sn, m_ref, l_ref, acc_ref, cn)
        o_ref[0] = acc_ref[...] / l_ref[...]


def _mla_sample(page_table, ql, qr, cn, kn, cache_lat, cache_rope_t, layer):
    seq_blk = lambda r, d: pl.BlockSpec((1, r, d), lambda s, c, pt: (s, 0, 0))
    grid_spec = pltpu.PrefetchScalarGridSpec(
        num_scalar_prefetch=1, grid=(DEC_BATCH, MLA_STEPS),
        in_specs=[seq_blk(Q_ROWS, MLA_KV_RANK), seq_blk(Q_ROWS, MLA_ROPE_DIM), seq_blk(DEC_SEQ, MLA_KV_RANK),
                  seq_blk(DEC_SEQ, MLA_ROPE_DIM), pl.BlockSpec(memory_space=pl.ANY), pl.BlockSpec(memory_space=pl.ANY)],
        out_specs=seq_blk(Q_ROWS, MLA_KV_RANK),
        scratch_shapes=[pltpu.VMEM((2, MLA_KEYS, MLA_KV_RANK), F32),
                        pltpu.VMEM((2, MLA_PAGES, MLA_ROPE_DIM, PAGE_SIZE), F32),
                        pltpu.SemaphoreType.DMA((2,)), pltpu.SemaphoreType.DMA((2,)),
                        pltpu.VMEM((Q_ROWS, 1), F32), pltpu.VMEM((Q_ROWS, 1), F32), pltpu.VMEM((Q_ROWS, MLA_KV_RANK), F32)])
    return pl.pallas_call(
        functools.partial(_mla_sample_kernel, layer=layer), grid_spec=grid_spec,
        out_shape=jax.ShapeDtypeStruct((DEC_BATCH, Q_ROWS, MLA_KV_RANK), F32),
        compiler_params=_cparams(2), name="mla_sample")(page_table, ql, qr, cn, kn, cache_lat, cache_rope_t)


N_MOBA_BLOCKS = PAST_LEN // MOBA_BLOCK
MOBA_CHUNK = 4096
MOBA_CHUNKS = PAST_LEN // MOBA_CHUNK
BLOCKS_PER_CHUNK = MOBA_CHUNK // MOBA_BLOCK


def _moba_sample_kernel(pt_ref, q_ref, kn_ref, vn_ref, k_hbm, v_hbm, o_ref, k_buf, v_buf, k_sem, v_sem, s_ref, p_ref,
                        *, layer):
    s = pl.program_id(0)
    slot = lax.rem(s, 2)

    def copies(hbm, buf, sem, seq, slot_):
        return [pltpu.make_async_copy(hbm.at[layer, pt_ref[seq, i]], buf.at[slot_, pl.ds(i * PAGE_SIZE, PAGE_SIZE)],
                                      sem.at[slot_]) for i in range(N_PAGES)]

    def start(seq, slot_):
        for cp in copies(k_hbm, k_buf, k_sem, seq, slot_):
            cp.start()
        for cp in copies(v_hbm, v_buf, v_sem, seq, slot_):
            cp.start()

    @pl.when(s == 0)
    def _():
        start(0, 0)

    @pl.when(s + 1 < DEC_BATCH)
    def _():
        start(s + 1, 1 - slot)

    for cp in copies(k_hbm, k_buf, k_sem, s, slot):
        cp.wait()

    qb = q_ref[0].astype(BF16)
    lane = lax.broadcasted_iota(jnp.int32, (Q_ROWS, 128), 1)
    gate = jnp.zeros((Q_ROWS, 128), F32)
    for ci in range(MOBA_CHUNKS):
        raw = _dot_nt(qb, k_buf[slot, ci * MOBA_CHUNK:(ci + 1) * MOBA_CHUNK, :].astype(BF16))
        s_ref[ci] = raw
        for blk in range(BLOCKS_PER_CHUNK):
            tot = jnp.sum(raw[:, blk * MOBA_BLOCK:(blk + 1) * MOBA_BLOCK], axis=-1, keepdims=True)
            gate = jnp.where(lane == ci * BLOCKS_PER_CHUNK + blk, tot * (1.0 / MOBA_BLOCK), gate)

    sel = _topk_select(gate, N_MOBA_BLOCKS, N_MOBA_BLOCKS).astype(BF16)
    sn = jnp.where(_new_token_mask(), _dot_nt(qb, kn_ref[0].astype(BF16)) * MOBA_SCALE, NEG_INF)
    m = jnp.max(sn, axis=-1, keepdims=True)
    key_blk = lax.broadcasted_iota(jnp.int32, (128, MOBA_CHUNK), 1) // MOBA_BLOCK
    blk_row = lax.broadcasted_iota(jnp.int32, (128, MOBA_CHUNK), 0)
    masked = []
    for ci in range(MOBA_CHUNKS):
        expand = (blk_row == key_blk + ci * BLOCKS_PER_CHUNK).astype(BF16)
        picked = _dot(sel, expand) > 0.5
        sc = jnp.where(picked, s_ref[ci] * MOBA_SCALE, NEG_INF)
        masked.append(sc)
        m = jnp.maximum(m, jnp.max(sc, axis=-1, keepdims=True))
    pn = jnp.exp(sn - m)
    l = jnp.sum(pn, axis=-1, keepdims=True)
    for ci in range(MOBA_CHUNKS):
        p = jnp.exp(masked[ci] - m)
        l = l + jnp.sum(p, axis=-1, keepdims=True)
        p_ref[ci] = p.astype(BF16)

    for cp in copies(v_hbm, v_buf, v_sem, s, slot):
        cp.wait()

    acc = _dot(pn.astype(BF16), vn_ref[0].astype(BF16))
    for ci in range(MOBA_CHUNKS):
        acc = acc + _dot(p_ref[ci], v_buf[slot, ci * MOBA_CHUNK:(ci + 1) * MOBA_CHUNK, :].astype(BF16))
    o_ref[0] = acc / l


def _moba_sample(page_table, q, kn, vn, cache_k, cache_v, layer):
    seq_blk = lambda r, d: pl.BlockSpec((1, r, d), lambda s, pt: (s, 0, 0))
    grid_spec = pltpu.PrefetchScalarGridSpec(
        num_scalar_prefetch=1, grid=(DEC_BATCH,),
        in_specs=[seq_blk(Q_ROWS, MOBA_HEAD_DIM), seq_blk(DEC_SEQ, MOBA_HEAD_DIM), seq_blk(DEC_SEQ, MOBA_HEAD_DIM),
                  pl.BlockSpec(memory_space=pl.ANY), pl.BlockSpec(memory_space=pl.ANY)],
        out_specs=seq_blk(Q_ROWS, MOBA_HEAD_DIM),
        scratch_shapes=[pltpu.VMEM((2, PAST_LEN, MOBA_HEAD_DIM), F32), pltpu.VMEM((2, PAST_LEN, MOBA_HEAD_DIM), F32),
                        pltpu.SemaphoreType.DMA((2,)), pltpu.SemaphoreType.DMA((2,)),
                        pltpu.VMEM((MOBA_CHUNKS, Q_ROWS, MOBA_CHUNK), F32),
                        pltpu.VMEM((MOBA_CHUNKS, Q_ROWS, MOBA_CHUNK), BF16)])
    return pl.pallas_call(
        functools.partial(_moba_sample_kernel, layer=layer), grid_spec=grid_spec,
        out_shape=jax.ShapeDtypeStruct((DEC_BATCH, Q_ROWS, MOBA_HEAD_DIM), F32),
        compiler_params=_cparams(1), name="moba_sample")(page_table, q, kn, vn, cache_k, cache_v)


def _rms(x, g):
    return x * lax.rsqrt(jnp.mean(x * x, -1, keepdims=True) + RMS_EPS) * g


def _rope(x, pos, inv_freq):
    ang = pos.astype(F32)[:, None] * inv_freq[None, :]
    shape = (x.shape[0],) + (1,) * (x.ndim - 2) + (inv_freq.shape[0],)
    cos = jnp.cos(ang).reshape(shape)
    sin = jnp.sin(ang).reshape(shape)
    x1, x2 = jnp.split(x, 2, axis=-1)
    return jnp.concatenate([x1 * cos - x2 * sin, x2 * cos + x1 * sin], axis=-1)


def _layer_weights(l, W):
    w_uq = W['mla_w_uq'][l]
    return dict(
        w_in_t=jnp.swapaxes(W['w_in'], 1, 2),
        b_gate=W['b_gate'][l].reshape(1, N_BRANCH * D_MODEL),
        w_uq=jnp.concatenate([w_uq[:, :, :MLA_NOPE_DIM].reshape(MLA_Q_RANK, -1),
                              w_uq[:, :, MLA_NOPE_DIM:].reshape(MLA_Q_RANK, -1)], axis=1).astype(BF16),
        w_uk_t=jnp.transpose(W['mla_w_uk'][l], (1, 2, 0)).astype(BF16),
        w_uv=jnp.transpose(W['mla_w_uv'][l], (1, 0, 2)).astype(BF16),
    )


def _mla_prep_kernel(cq_ref, kv_ref, qg_ref, kg_ref, wuq_ref, wuk_ref, cos_ref, sin_ref,
                     ql_ref, qr_ref, c_ref, cbf_ref, kr_ref, krbf_ref):
    cq = cq_ref[...]
    cqn = cq * lax.rsqrt(jnp.mean(cq * cq, axis=-1, keepdims=True) + RMS_EPS) * qg_ref[...]
    qm = _dot(cqn.astype(BF16), wuq_ref[...])
    cos, sin = cos_ref[...], sin_ref[...]
    for h in range(MLA_HEADS):
        ql_ref[h] = _dot(qm[:, h * MLA_NOPE_DIM:(h + 1) * MLA_NOPE_DIM].astype(BF16), wuk_ref[h]).astype(BF16)
    rope0 = MLA_HEADS * MLA_NOPE_DIM
    for j in range(MLA_HEADS // 2):
        pair = _rope128(qm[:, rope0 + j * 128:rope0 + (j + 1) * 128], cos, sin)
        qr_ref[2 * j] = pair[:, :MLA_ROPE_DIM].astype(BF16)
        qr_ref[2 * j + 1] = pair[:, MLA_ROPE_DIM:].astype(BF16)
    ckv = kv_ref[:, :MLA_KV_RANK]
    c = ckv * lax.rsqrt(jnp.mean(ckv * ckv, axis=-1, keepdims=True) + RMS_EPS) * kg_ref[...]
    c_ref[...] = c
    cbf_ref[...] = c.astype(BF16)
    kr = _rope128(kv_ref[:, MLA_KV_RANK:], cos, sin)[:, :MLA_ROPE_DIM]
    kr_ref[...] = kr
    krbf_ref[...] = kr.astype(BF16)


def _mla_prep(proj, pos, l, W, LW, tm=512):
    m = proj.shape[0]
    mla_inv = 1.0 / (ROPE_THETA ** (jnp.arange(0, MLA_ROPE_DIM, 2, dtype=F32) / MLA_ROPE_DIM))
    cos, sin = _rope_tables(pos, mla_inv)
    n_tab = cos.shape[0] // tm
    const = lambda shape: pl.BlockSpec(shape, lambda i: (0,) * len(shape))
    tab = pl.BlockSpec((tm, 128), lambda i: (i % n_tab, 0))
    width = MLA_HEADS * (MLA_NOPE_DIM + MLA_ROPE_DIM)
    return pl.pallas_call(
        _mla_prep_kernel, grid=(m // tm,),
        in_specs=[pl.BlockSpec((tm, MLA_Q_RANK), lambda i: (i, 2304 // MLA_Q_RANK)),
                  pl.BlockSpec((tm, 384), lambda i: (i, 2688 // 384)),
                  const((1, MLA_Q_RANK)), const((1, MLA_KV_RANK)), const((MLA_Q_RANK, width)),
                  const((MLA_HEADS, MLA_NOPE_DIM, MLA_KV_RANK)), tab, tab],
        out_specs=[pl.BlockSpec((MLA_HEADS, tm, MLA_KV_RANK), lambda i: (0, i, 0)),
                   pl.BlockSpec((MLA_HEADS, tm, MLA_ROPE_DIM), lambda i: (0, i, 0)),
                   pl.BlockSpec((tm, MLA_KV_RANK), lambda i: (i, 0)), pl.BlockSpec((tm, MLA_KV_RANK), lambda i: (i, 0)),
                   pl.BlockSpec((tm, MLA_ROPE_DIM), lambda i: (i, 0)), pl.BlockSpec((tm, MLA_ROPE_DIM), lambda i: (i, 0))],
        out_shape=[jax.ShapeDtypeStruct((MLA_HEADS, m, MLA_KV_RANK), BF16),
                   jax.ShapeDtypeStruct((MLA_HEADS, m, MLA_ROPE_DIM), BF16),
                   jax.ShapeDtypeStruct((m, MLA_KV_RANK), F32), jax.ShapeDtypeStruct((m, MLA_KV_RANK), BF16),
                   jax.ShapeDtypeStruct((m, MLA_ROPE_DIM), F32), jax.ShapeDtypeStruct((m, MLA_ROPE_DIM), BF16)],
        compiler_params=_cparams(1), name="mla_prep")(
            proj, proj, W['mla_q_norm_g'][l].reshape(1, -1), W['mla_kv_norm_g'][l].reshape(1, -1),
            LW['w_uq'], LW['w_uk_t'], cos, sin)


def _project(x_bf, pos, l, W, LW, tm):
    proj = _proj(x_bf, LW['w_in_t'], l, tm)
    p = {'proj': proj, 'moba_k': proj[:, 2048:2176], 'moba_v': proj[:, 2176:2304]}
    (p['mla_q_lat'], p['mla_q_rope'], p['mla_c'], p['mla_c_bf'], p['mla_kr'], p['mla_kr_bf']) = _mla_prep(proj, pos, l, W, LW)
    return p


def _sample_mixer_inputs(proj, pos, l, W):
    m = proj.shape[0]
    ret_inv = 1.0 / (ROPE_THETA ** jnp.linspace(0.0, 1.0, RET_KEY_DIM // 2, dtype=F32))
    p = {}
    p['ret_q'] = _rope(proj[:, 0:256].reshape(m, RET_HEADS, RET_KEY_DIM), pos, ret_inv)
    p['ret_k'] = _rope(proj[:, 256:512].reshape(m, RET_HEADS, RET_KEY_DIM), pos, ret_inv) * (RET_KEY_DIM ** -0.5)
    p['ret_v'] = proj[:, 512:1024].reshape(m, RET_HEADS, RET_VAL_DIM)
    p['ret_g'] = proj[:, 1024:1536]
    p['moba_q'] = proj[:, 1536:2048]
    ga = proj[:, SLAB_GA:SLAB_GA + GLA_GATE_RANK]
    p['gla_q'] = proj[:, 3072:3328].reshape(m, GLA_HEADS, GLA_KEY_DIM) * (GLA_KEY_DIM ** -0.5)
    p['gla_k'] = proj[:, 3328:3584].reshape(m, GLA_HEADS, GLA_KEY_DIM)
    p['gla_v'] = proj[:, 3584:4096].reshape(m, GLA_HEADS, GLA_VAL_DIM)
    a_logit = _matmul(ga.astype(BF16), W['gla_w_a'][l].reshape(GLA_GATE_RANK, -1).astype(BF16), m, 256, F32)
    a_logit = a_logit.reshape(m, GLA_HEADS, GLA_KEY_DIM) + W['gla_b_a'][l]
    p['gla_log_a'] = jax.nn.log_sigmoid(a_logit) / GLA_GATE_TAU
    p['gla_r'] = proj[:, SLAB_GR:SLAB_GR + BRANCH_WIDTH]
    return p


def _finish_layer(x, x_bf, l, W, LW, branch_list, tm):
    branches = jnp.stack(branch_list, axis=0)
    merged = _merge(x_bf, LW['w_in_t'], LW['b_gate'], branches, W['w_branch'], l, tm, 512)
    x1, x1_bf = _mm_res_ln(merged, W['w_out'], l, x, W['ln1_g'][l], W['ln1_b'][l], 512, 512)
    hidden = _ffn_up(x1_bf, W['ffn_w_gate'], W['ffn_w_up'], l, tm, 512)
    return _mm_res_ln(hidden, W['ffn_w_down'], l, x1, W['ln2_g'][l], W['ln2_b'][l], 512, 512)


def _prompt_layer(x, x_bf, l, W, LW):
    m = BATCH * SEQ
    p = _project(x_bf, jnp.arange(SEQ, dtype=jnp.int32), l, W, LW, 1024)
    ret_o, ret_s = _ret_prompt(p['proj'], W['ret_norm_g'][l])
    gla_o, gla_st = _gla_prompt(p['proj'], W['gla_w_a'][l], W['gla_b_a'][l], W['gla_norm_g'][l])
    moba_o = _moba_prompt(p['proj'])
    mla_o = _mla_prompt(p['mla_q_lat'], p['mla_q_rope'], p['mla_c_bf'], p['mla_kr_bf'], LW['w_uv'])
    x, x_bf = _finish_layer(x, x_bf, l, W, LW, [ret_o, moba_o, mla_o, gla_o], 1024)
    new = (p['moba_k'].reshape(BATCH, SEQ, 1, MOBA_HEAD_DIM), p['moba_v'].reshape(BATCH, SEQ, 1, MOBA_HEAD_DIM),
           p['mla_c'].reshape(BATCH, SEQ, MLA_KV_RANK), p['mla_kr'].reshape(BATCH, SEQ, MLA_ROPE_DIM),
           ret_s, jnp.swapaxes(gla_st, 2, 3))
    return x, x_bf, new


def _sample_layer(x, x_bf, l, W, LW, caches, page_table):
    m = DEC_BATCH * DEC_SEQ
    pos = jnp.tile(PAST_LEN + jnp.arange(DEC_SEQ, dtype=jnp.int32), DEC_BATCH)
    p = _project(x_bf, pos, l, W, LW, 512)
    p.update(_sample_mixer_inputs(p['proj'], pos, l, W))
    seq_heads = lambda t, d: jnp.swapaxes(t.reshape(DEC_BATCH, DEC_SEQ, N_HEADS, d), 1, 2)
    back = lambda t, d: jnp.swapaxes(t, 1, 2).reshape(m, N_HEADS * d)

    ret_k = seq_heads(p['ret_k'], RET_KEY_DIM)
    ret_o, ret_s = _ret_sample(seq_heads(p['ret_q'], RET_KEY_DIM), jnp.swapaxes(ret_k, 2, 3),
                               seq_heads(p['ret_v'], RET_VAL_DIM), caches['state_retention'][l])
    gla_o, gla_s = _gla_sample(seq_heads(p['gla_q'], GLA_KEY_DIM), seq_heads(p['gla_k'], GLA_KEY_DIM),
                               seq_heads(p['gla_v'], GLA_VAL_DIM), seq_heads(p['gla_log_a'], GLA_KEY_DIM),
                               caches['state_gla'][l])
    moba_q = seq_heads(p['moba_q'], MOBA_HEAD_DIM).reshape(DEC_BATCH, Q_ROWS, MOBA_HEAD_DIM)
    moba_o = _moba_sample(page_table, moba_q, p['moba_k'].reshape(DEC_BATCH, DEC_SEQ, MOBA_HEAD_DIM),
                          p['moba_v'].reshape(DEC_BATCH, DEC_SEQ, MOBA_HEAD_DIM),
                          caches['cache_moba_k'].reshape(DEPTH, -1, PAGE_SIZE, MOBA_HEAD_DIM),
                          caches['cache_moba_v'].reshape(DEPTH, -1, PAGE_SIZE, MOBA_HEAD_DIM), l)
    moba_o = back(moba_o.reshape(DEC_BATCH, N_HEADS, DEC_SEQ, MOBA_HEAD_DIM), MOBA_HEAD_DIM)

    to_rows = lambda t, d: jnp.swapaxes(t.reshape(N_HEADS, DEC_BATCH, DEC_SEQ, d), 0, 1).reshape(DEC_BATCH, Q_ROWS, d)
    mla_lat = _mla_sample(page_table, to_rows(p['mla_q_lat'], MLA_KV_RANK), to_rows(p['mla_q_rope'], MLA_ROPE_DIM),
                          p['mla_c'].reshape(DEC_BATCH, DEC_SEQ, MLA_KV_RANK),
                          p['mla_kr'].reshape(DEC_BATCH, DEC_SEQ, MLA_ROPE_DIM),
                          caches['cache_mla_latent'], jnp.swapaxes(caches['cache_mla_rope'], 2, 3), l)
    mla_lat = jnp.swapaxes(mla_lat.reshape(DEC_BATCH, N_HEADS, DEC_SEQ, MLA_KV_RANK), 0, 1).reshape(N_HEADS, m, MLA_KV_RANK)
    mla_o = _matmul_heads(mla_lat.astype(BF16), LW['w_uv'], 512, BF16)
    mla_o = jnp.swapaxes(mla_o, 0, 1).reshape(m, BRANCH_WIDTH)

    ret = (_rms(jnp.swapaxes(ret_o, 1, 2), W['ret_norm_g'][l]).reshape(m, BRANCH_WIDTH) * jax.nn.silu(p['ret_g']))
    gla = (_rms(jnp.swapaxes(gla_o, 1, 2), W['gla_norm_g'][l]).reshape(m, BRANCH_WIDTH) * jax.nn.silu(p['gla_r']))
    x, x_bf = _finish_layer(x, x_bf, l, W, LW, [ret.astype(BF16), moba_o.astype(BF16), mla_o, gla.astype(BF16)], 512)
    new = (p['moba_k'].reshape(DEC_BATCH, DEC_SEQ, 1, MOBA_HEAD_DIM), p['moba_v'].reshape(DEC_BATCH, DEC_SEQ, 1, MOBA_HEAD_DIM),
           p['mla_c'].reshape(DEC_BATCH, DEC_SEQ, MLA_KV_RANK), p['mla_kr'].reshape(DEC_BATCH, DEC_SEQ, MLA_ROPE_DIM),
           ret_s, gla_s)
    return x, x_bf, new


def kernel(x_prompt, x_sample, cache_moba_k, cache_moba_v, cache_mla_latent, cache_mla_rope, state_retention, state_gla, page_table, w_in, b_gate, ret_norm_g, mla_q_norm_g, mla_w_uq, mla_kv_norm_g, mla_w_uk, mla_w_uv, gla_w_a, gla_b_a, gla_norm_g, w_branch, w_out, ln1_g, ln1_b, ffn_w_gate, ffn_w_up, ffn_w_down, ln2_g, ln2_b):
    W = dict(w_in=w_in, b_gate=b_gate, ret_norm_g=ret_norm_g, mla_q_norm_g=mla_q_norm_g, mla_w_uq=mla_w_uq,
             mla_kv_norm_g=mla_kv_norm_g, mla_w_uk=mla_w_uk, mla_w_uv=mla_w_uv, gla_w_a=gla_w_a, gla_b_a=gla_b_a,
             gla_norm_g=gla_norm_g, w_branch=w_branch, w_out=w_out, ln1_g=ln1_g, ln1_b=ln1_b,
             ffn_w_gate=ffn_w_gate, ffn_w_up=ffn_w_up, ffn_w_down=ffn_w_down, ln2_g=ln2_g, ln2_b=ln2_b)
    caches = dict(cache_moba_k=cache_moba_k, cache_moba_v=cache_moba_v, cache_mla_latent=cache_mla_latent,
                  cache_mla_rope=cache_mla_rope, state_retention=state_retention, state_gla=state_gla)
    xp = x_prompt.reshape(BATCH * SEQ, D_MODEL)
    xs = x_sample.reshape(DEC_BATCH * DEC_SEQ, D_MODEL)
    xp_bf, xs_bf = xp.astype(BF16), xs.astype(BF16)
    new_p, new_s = [], []
    for l in range(DEPTH):
        LW = _layer_weights(l, W)
        xp, xp_bf, st_p = _prompt_layer(xp, xp_bf, l, W, LW)
        xs, xs_bf, st_s = _sample_layer(xs, xs_bf, l, W, LW, caches, page_table)
        new_p.append(st_p)
        new_s.append(st_s)
    stack = lambda states, i: jnp.stack([s[i] for s in states], axis=0)
    return ((xp.reshape(BATCH, SEQ, D_MODEL), xs.reshape(DEC_BATCH, DEC_SEQ, D_MODEL))
            + tuple(stack(new_p, i) for i in range(6)) + tuple(stack(new_s, i) for i in range(6)))
```

```python
import functools

import numpy as np
import jax
import jax.numpy as jnp
from jax import lax
from jax.experimental import pallas as pl
from jax.experimental.pallas import tpu as pltpu

D_MODEL = 2048
BATCH = 4
SEQ = 2048
DEPTH = 2
DEC_BATCH = 128
DEC_SEQ = 4
PAST_LEN = 16384
PAGE_SIZE = 128
N_PAGES = PAST_LEN // PAGE_SIZE

RET_HEADS = 4
RET_KEY_DIM = 64
RET_VAL_DIM = 128
MOBA_HEADS = 4
MOBA_HEAD_DIM = 128
MOBA_BLOCK = 256
MOBA_TOPK = 3
MLA_HEADS = 4
MLA_Q_RANK = 384
MLA_KV_RANK = 256
MLA_NOPE_DIM = 128
MLA_ROPE_DIM = 64
MLA_V_DIM = 128
GLA_HEADS = 4
GLA_KEY_DIM = 64
GLA_VAL_DIM = 128
GLA_GATE_RANK = 16
GLA_GATE_TAU = 16.0
N_BRANCH = 4
BRANCH_WIDTH = 512
FFN_HIDDEN = 5632
ROPE_THETA = 10000.0
DN_ALPHA = (2 * DEPTH) ** 0.25
LN_EPS = 1e-5
RMS_EPS = 1e-6
F32 = jnp.float32
BF16 = jnp.bfloat16

N_HEADS = 4
MAIN_COLS = 5120
SLAB_GA = 4096
SLAB_GR = 4608
GATE_COL0 = 4560
MLA_SCALE = (MLA_NOPE_DIM + MLA_ROPE_DIM) ** -0.5
MOBA_SCALE = MOBA_HEAD_DIM ** -0.5

VMEM_LIMIT = 56 * 1024 * 1024
NEG_INF = float("-inf")


def _cparams(n_axes):
    return pltpu.CompilerParams(dimension_semantics=("arbitrary",) * n_axes, vmem_limit_bytes=VMEM_LIMIT)


def _dot(a, b):
    return jnp.dot(a, b, preferred_element_type=F32)


def _dot_nt(a, b):
    return lax.dot_general(a, b, (((1,), (1,)), ((), ())), preferred_element_type=F32)


def _dot_tn(a, b):
    return lax.dot_general(a, b, (((0,), (0,)), ((), ())), preferred_element_type=F32)


def _sigmoid(x):
    return 1.0 / (1.0 + jnp.exp(-x))


def _mm_kernel(a_ref, b_ref, o_ref):
    o_ref[...] = _dot(a_ref[...], b_ref[...]).astype(o_ref.dtype)


def _matmul(a, b, tm, tn, out_dtype):
    m, k = a.shape
    n = b.shape[1]
    return pl.pallas_call(
        _mm_kernel, grid=(m // tm, n // tn),
        in_specs=[pl.BlockSpec((tm, k), lambda i, j: (i, 0)), pl.BlockSpec((k, tn), lambda i, j: (0, j))],
        out_specs=pl.BlockSpec((tm, tn), lambda i, j: (i, j)),
        out_shape=jax.ShapeDtypeStruct((m, n), out_dtype),
        compiler_params=_cparams(2), name="mm")(a, b)


def _proj_kernel(x_ref, w_ref, o_ref, w_bf):
    @pl.when(pl.program_id(1) == 0)
    def _():
        w_bf[...] = w_ref[0].astype(BF16)

    o_ref[...] = _dot_nt(x_ref[...], w_bf[...])


def _slab_col_start(j):
    return jnp.where(j < 6, j * 512, jnp.where(j < 9, 3008 + (j - 6) * 512, 4048))


def _proj(x, w_in_t, layer, tm):
    m = x.shape[0]
    return pl.pallas_call(
        _proj_kernel, grid=(MAIN_COLS // 512, m // tm),
        in_specs=[pl.BlockSpec((tm, D_MODEL), lambda j, i: (i, 0)),
                  pl.BlockSpec((pl.Element(1), pl.Element(512), pl.Element(D_MODEL)),
                               lambda j, i: (layer, pl.multiple_of(_slab_col_start(j), 8), 0))],
        out_specs=pl.BlockSpec((tm, 512), lambda j, i: (i, j)),
        out_shape=jax.ShapeDtypeStruct((m, MAIN_COLS), F32),
        scratch_shapes=[pltpu.VMEM((512, D_MODEL), BF16)],
        compiler_params=_cparams(2), name="proj")(x, w_in_t)


def _mm_heads_kernel(a_ref, b_ref, o_ref):
    o_ref[0] = _dot(a_ref[0], b_ref[0]).astype(o_ref.dtype)


def _matmul_heads(a, b, tm, out_dtype):
    h, m, k = a.shape
    n = b.shape[2]
    return pl.pallas_call(
        _mm_heads_kernel, grid=(h, m // tm),
        in_specs=[pl.BlockSpec((1, tm, k), lambda g, i: (g, i, 0)), pl.BlockSpec((1, k, n), lambda g, i: (g, 0, 0))],
        out_specs=pl.BlockSpec((1, tm, n), lambda g, i: (g, i, 0)),
        out_shape=jax.ShapeDtypeStruct((h, m, n), out_dtype),
        compiler_params=_cparams(2), name="mm_heads")(a, b)


def _ffn_up_kernel(x_ref, wg_ref, wu_ref, o_ref, wg_bf, wu_bf):
    @pl.when(pl.program_id(1) == 0)
    def _():
        wg_bf[...] = wg_ref[...].astype(BF16)
        wu_bf[...] = wu_ref[...].astype(BF16)

    x = x_ref[...]
    g = _dot(x, wg_bf[...])
    u = _dot(x, wu_bf[...])
    o_ref[...] = (g * _sigmoid(g) * u).astype(o_ref.dtype)


def _ffn_up(x, wg, wu, layer, tm, tn):
    m, k = x.shape
    n = wg.shape[2]
    w_spec = pl.BlockSpec((None, k, tn), lambda j, i: (layer, 0, j))
    return pl.pallas_call(
        _ffn_up_kernel, grid=(n // tn, m // tm),
        in_specs=[pl.BlockSpec((tm, k), lambda j, i: (i, 0)), w_spec, w_spec],
        out_specs=pl.BlockSpec((tm, tn), lambda j, i: (i, j)),
        out_shape=jax.ShapeDtypeStruct((m, n), BF16),
        scratch_shapes=[pltpu.VMEM((k, tn), BF16), pltpu.VMEM((k, tn), BF16)],
        compiler_params=_cparams(2), name="ffn_up")(x, wg, wu)


def _mm_res_ln_kernel(a_ref, b_ref, res_ref, g_ref, beta_ref, o_ref, obf_ref, acc_ref):
    k = pl.program_id(1)

    @pl.when(k == 0)
    def _():
        acc_ref[...] = jnp.zeros_like(acc_ref)

    acc_ref[...] += _dot(a_ref[...], b_ref[...].astype(BF16))

    @pl.when(k == pl.num_programs(1) - 1)
    def _():
        y = DN_ALPHA * res_ref[...] + acc_ref[...]
        mu = jnp.mean(y, axis=-1, keepdims=True)
        yc = y - mu
        var = jnp.mean(yc * yc, axis=-1, keepdims=True)
        out = yc * lax.rsqrt(var + LN_EPS) * g_ref[...] + beta_ref[...]
        o_ref[...] = out
        obf_ref[...] = out.astype(BF16)


def _mm_res_ln(a, b, layer, res, g, beta, tm, tk):
    m, k = a.shape
    n = b.shape[2]
    return pl.pallas_call(
        _mm_res_ln_kernel, grid=(m // tm, k // tk),
        in_specs=[pl.BlockSpec((tm, tk), lambda i, j: (i, j)), pl.BlockSpec((None, tk, n), lambda i, j: (layer, j, 0)),
                  pl.BlockSpec((tm, n), lambda i, j: (i, 0)), pl.BlockSpec((1, n), lambda i, j: (0, 0)),
                  pl.BlockSpec((1, n), lambda i, j: (0, 0))],
        out_specs=[pl.BlockSpec((tm, n), lambda i, j: (i, 0)), pl.BlockSpec((tm, n), lambda i, j: (i, 0))],
        out_shape=[jax.ShapeDtypeStruct((m, n), F32), jax.ShapeDtypeStruct((m, n), BF16)],
        scratch_shapes=[pltpu.VMEM((tm, n), F32)],
        compiler_params=_cparams(2), name="mm_res_ln")(a, b, res, g.reshape(1, n), beta.reshape(1, n))


def _merge_kernel(x_ref, wg_ref, bg_ref, br_ref, wb_ref, o_ref, acc_ref):
    n = pl.program_id(2)
    gate = _sigmoid(_dot_nt(x_ref[...], wg_ref[0].astype(BF16)) + bg_ref[...])
    contrib = gate * _dot(br_ref[0], wb_ref[...].astype(BF16))

    @pl.when(n == 0)
    def _():
        acc_ref[...] = contrib

    @pl.when(n > 0)
    def _():
        acc_ref[...] += contrib

    @pl.when(n == N_BRANCH - 1)
    def _():
        o_ref[...] = acc_ref[...].astype(o_ref.dtype)


def _merge(x, w_in_t, b_gate, branches, w_branch, layer, tm, td):
    m = x.shape[0]
    nd = D_MODEL // td
    return pl.pallas_call(
        _merge_kernel, grid=(m // tm, nd, N_BRANCH),
        in_specs=[pl.BlockSpec((tm, D_MODEL), lambda i, d, n: (i, 0)),
                  pl.BlockSpec((pl.Element(1), pl.Element(td), pl.Element(D_MODEL)),
                               lambda i, d, n: (layer, pl.multiple_of(GATE_COL0 + (n * nd + d) * td, 8), 0)),
                  pl.BlockSpec((1, td), lambda i, d, n: (0, n * nd + d)),
                  pl.BlockSpec((1, tm, BRANCH_WIDTH), lambda i, d, n: (n, i, 0)),
                  pl.BlockSpec((None, None, BRANCH_WIDTH, td), lambda i, d, n: (layer, n, 0, d))],
        out_specs=pl.BlockSpec((tm, td), lambda i, d, n: (i, d)),
        out_shape=jax.ShapeDtypeStruct((m, D_MODEL), BF16),
        scratch_shapes=[pltpu.VMEM((tm, td), F32)],
        compiler_params=_cparams(3), name="merge")(x, w_in_t, b_gate, branches, w_branch)


def _softmax_step(s, m_ref, l_ref, acc_ref, v_bf):
    m_prev = m_ref[...]
    m_new = jnp.maximum(m_prev, jnp.max(s, axis=-1, keepdims=True))
    alpha = jnp.exp(m_prev - m_new)
    p = jnp.exp(s - m_new)
    l_ref[...] = alpha * l_ref[...] + jnp.sum(p, axis=-1, keepdims=True)
    acc_ref[...] = alpha * acc_ref[...] + _dot(p.astype(BF16), v_bf)
    m_ref[...] = m_new


def _mla_prompt_kernel(ql_ref, qr_ref, c_ref, kr_ref, wuv_ref, o_ref, m_ref, l_ref, acc_ref, *, tq, tk):
    qi = pl.program_id(1)
    kj = pl.program_id(2)
    rows = N_HEADS * tq

    @pl.when(kj == 0)
    def _():
        m_ref[...] = jnp.full_like(m_ref, NEG_INF)
        l_ref[...] = jnp.zeros_like(l_ref)
        acc_ref[...] = jnp.zeros_like(acc_ref)

    @pl.when(kj * tk <= qi * tq + tq - 1)
    def _():
        ql = ql_ref[...].reshape(rows, MLA_KV_RANK)
        qr = qr_ref[...].reshape(rows, MLA_ROPE_DIM)
        c = c_ref[...]
        s = (_dot_nt(ql, c) + _dot_nt(qr, kr_ref[...])) * MLA_SCALE
        qpos = qi * tq + (lax.broadcasted_iota(jnp.int32, (rows, tk), 0) & (tq - 1))
        kpos = kj * tk + lax.broadcasted_iota(jnp.int32, (rows, tk), 1)
        s = jnp.where(kpos <= qpos, s, NEG_INF)
        _softmax_step(s, m_ref, l_ref, acc_ref, c)

    @pl.when(kj == pl.num_programs(2) - 1)
    def _():
        o = (acc_ref[...] / l_ref[...]).astype(BF16)
        for h in range(N_HEADS):
            o_ref[:, h * MLA_V_DIM:(h + 1) * MLA_V_DIM] = _dot(o[h * tq:(h + 1) * tq], wuv_ref[h]).astype(o_ref.dtype)


def _mla_prompt(ql, qr, c, kr, wuv, tq=256, tk=512):
    nq = SEQ // tq
    nk = SEQ // tk

    def kv_map(b, i, j):
        return (b * nk + jnp.minimum(j, (i * tq + tq - 1) // tk), 0)

    return pl.pallas_call(
        functools.partial(_mla_prompt_kernel, tq=tq, tk=tk), grid=(BATCH, nq, nk),
        in_specs=[pl.BlockSpec((N_HEADS, tq, MLA_KV_RANK), lambda b, i, j: (0, b * nq + i, 0)),
                  pl.BlockSpec((N_HEADS, tq, MLA_ROPE_DIM), lambda b, i, j: (0, b * nq + i, 0)),
                  pl.BlockSpec((tk, MLA_KV_RANK), kv_map),
                  pl.BlockSpec((tk, MLA_ROPE_DIM), kv_map),
                  pl.BlockSpec((N_HEADS, MLA_KV_RANK, MLA_V_DIM), lambda b, i, j: (0, 0, 0))],
        out_specs=pl.BlockSpec((tq, BRANCH_WIDTH), lambda b, i, j: (b * nq + i, 0)),
        out_shape=jax.ShapeDtypeStruct((BATCH * SEQ, BRANCH_WIDTH), BF16),
        scratch_shapes=[pltpu.VMEM((N_HEADS * tq, 1), F32), pltpu.VMEM((N_HEADS * tq, 1), F32),
                        pltpu.VMEM((N_HEADS * tq, MLA_KV_RANK), F32)],
        compiler_params=_cparams(3), name="mla_prompt")(ql, qr, c, kr, wuv)


def _split_bf16(x):
    hi = x.astype(BF16)
    lo = (x - hi.astype(F32)).astype(BF16)
    return hi, lo


def _topk_select(gate, n_valid, n_cand):
    lane = lax.broadcasted_iota(jnp.int32, gate.shape, 1)
    gm = jnp.where(lane < n_valid, gate, NEG_INF)
    rank = jnp.zeros(gate.shape, jnp.int32)
    for a in range(n_cand):
        ga = gm[:, a:a + 1]
        beats = (ga > gm) | ((ga == gm) & (lane > a))
        rank = rank + beats.astype(jnp.int32)
    return (rank < MOBA_TOPK) & (lane < n_valid)


def _moba_prompt_kernel(q_ref, k_ref, v_ref, o_ref, m_ref, l_ref, acc_ref, *, tq):
    i = pl.program_id(1)
    rows = N_HEADS * tq
    nb = SEQ // MOBA_BLOCK
    qf = jnp.concatenate([q_ref[:, h * MOBA_HEAD_DIM:(h + 1) * MOBA_HEAD_DIM] for h in range(N_HEADS)], axis=0)
    qb = qf.astype(BF16)

    means = jnp.sum(k_ref[...].reshape(nb, MOBA_BLOCK, MOBA_HEAD_DIM), axis=1) * (1.0 / MOBA_BLOCK)
    means = jnp.concatenate([means, jnp.zeros((128 - nb, MOBA_HEAD_DIM), F32)], axis=0)
    q_hi, q_lo = _split_bf16(qf)
    mn_hi, mn_lo = _split_bf16(means)
    gate = _dot_nt(q_hi, mn_hi) + _dot_nt(q_hi, mn_lo) + _dot_nt(q_lo, mn_hi)
    sel = _topk_select(gate, i, nb).astype(F32)
    lane = lax.broadcasted_iota(jnp.int32, (rows, 128), 1)

    own0 = pl.multiple_of(i * MOBA_BLOCK, MOBA_BLOCK)
    s = _dot_nt(qb, k_ref[pl.ds(own0, MOBA_BLOCK), :].astype(BF16)) * MOBA_SCALE
    qrel = lax.broadcasted_iota(jnp.int32, (rows, MOBA_BLOCK), 0) & (tq - 1)
    krel = lax.broadcasted_iota(jnp.int32, (rows, MOBA_BLOCK), 1)
    s = jnp.where(krel <= qrel, s, NEG_INF)
    m_ref[...] = jnp.full_like(m_ref, NEG_INF)
    l_ref[...] = jnp.zeros_like(l_ref)
    acc_ref[...] = jnp.zeros_like(acc_ref)
    _softmax_step(s, m_ref, l_ref, acc_ref, v_ref[pl.ds(own0, MOBA_BLOCK), :].astype(BF16))

    def past_block(j, carry):
        j0 = pl.multiple_of(j * MOBA_BLOCK, MOBA_BLOCK)
        sj = _dot_nt(qb, k_ref[pl.ds(j0, MOBA_BLOCK), :].astype(BF16)) * MOBA_SCALE
        picked = jnp.max(jnp.where(lane == j, sel, 0.0), axis=-1, keepdims=True) > 0.0
        sj = jnp.where(picked, sj, NEG_INF)
        _softmax_step(sj, m_ref, l_ref, acc_ref, v_ref[pl.ds(j0, MOBA_BLOCK), :].astype(BF16))
        return carry

    lax.fori_loop(0, i, past_block, 0)
    o = acc_ref[...] / l_ref[...]
    for h in range(N_HEADS):
        o_ref[:, h * MOBA_HEAD_DIM:(h + 1) * MOBA_HEAD_DIM] = o[h * tq:(h + 1) * tq].astype(o_ref.dtype)


def _moba_prompt(proj):
    tq = MOBA_BLOCK
    nq = SEQ // tq
    return pl.pallas_call(
        functools.partial(_moba_prompt_kernel, tq=tq), grid=(BATCH, nq),
        in_specs=[pl.BlockSpec((tq, 512), lambda b, i: (b * nq + i, 1536 // 512)),
                  pl.BlockSpec((SEQ, 128), lambda b, i: (b, 2048 // 128)),
                  pl.BlockSpec((SEQ, 128), lambda b, i: (b, 2176 // 128))],
        out_specs=pl.BlockSpec((tq, BRANCH_WIDTH), lambda b, i: (b * nq + i, 0)),
        out_shape=jax.ShapeDtypeStruct((BATCH * SEQ, BRANCH_WIDTH), BF16),
        scratch_shapes=[pltpu.VMEM((N_HEADS * tq, 1), F32), pltpu.VMEM((N_HEADS * tq, 1), F32),
                        pltpu.VMEM((N_HEADS * tq, MOBA_HEAD_DIM), F32)],
        compiler_params=_cparams(2), name="moba_prompt")(proj, proj, proj)


def _ret_log_gamma():
    return np.log1p(-np.exp2(-5.0 - np.arange(RET_HEADS, dtype=np.float32))).astype(np.float32)


def _ret_tables(chunk):
    log_g = jnp.asarray(_ret_log_gamma())
    idx = jnp.arange(chunk, dtype=F32)
    diff = idx[:, None] - idx[None, :]
    decay = jnp.where(diff >= 0, jnp.exp(jnp.maximum(diff, 0.0)[None] * log_g[:, None, None]), 0.0)
    q_dec = jnp.exp((idx + 1.0)[None, :] * log_g[:, None])[..., None]
    k_dec = jnp.exp((chunk - 1.0 - idx)[None, :] * log_g[:, None])[:, None, :]
    c_dec = [float(np.exp(np.float32(chunk) * g)) for g in _ret_log_gamma()]
    return decay, q_dec, k_dec, c_dec


def _rope_tables(pos, inv_freq):
    ang = pos.astype(F32)[:, None] * inv_freq[None, :]
    cos, sin = jnp.cos(ang), jnp.sin(ang)
    return jnp.tile(cos, (1, 4)), jnp.tile(jnp.concatenate([-sin, sin], axis=1), (1, 2))


def _rope128(x, cos, sin_signed):
    lane = lax.broadcasted_iota(jnp.int32, x.shape, 1)
    partner = jnp.where((lane & 63) < 32, pltpu.roll(x, 96, 1), pltpu.roll(x, 32, 1))
    return x * cos + partner * sin_signed


def _head_rms_gate(o, norm_g, gate):
    o = o * lax.rsqrt(jnp.mean(o * o, axis=-1, keepdims=True) + RMS_EPS) * norm_g
    return o * (gate * _sigmoid(gate))


def _ret_prompt_kernel(qk_ref, v_ref, g_ref, cos_ref, sin_ref, dec_ref, qd_ref, kd_ref, ng_ref, o_ref, s_ref, st_ref,
                       *, c_dec):
    c = pl.program_id(1)

    @pl.when(c == 0)
    def _():
        st_ref[...] = jnp.zeros_like(st_ref)

    cos, sin = cos_ref[...], sin_ref[...]
    roped = [_rope128(qk_ref[:, j * 128:(j + 1) * 128], cos, sin) for j in range(4)]
    for h in range(RET_HEADS):
        lo = (h % 2) * RET_KEY_DIM
        q = roped[h // 2][:, lo:lo + RET_KEY_DIM]
        k = roped[2 + h // 2][:, lo:lo + RET_KEY_DIM] * (RET_KEY_DIM ** -0.5)
        v = v_ref[:, h * RET_VAL_DIM:(h + 1) * RET_VAL_DIM].astype(BF16)
        state = st_ref[h]
        att = _dot_nt(q.astype(BF16), k.astype(BF16)) * dec_ref[h]
        o = _dot(att.astype(BF16), v) + _dot((q * qd_ref[h]).astype(BF16), state.astype(BF16))
        st_ref[h] = state * c_dec[h] + _dot_tn((k * kd_ref[h]).astype(BF16), v)
        cols = slice(h * RET_VAL_DIM, (h + 1) * RET_VAL_DIM)
        o_ref[:, cols] = _head_rms_gate(o, ng_ref[:, cols], g_ref[:, cols]).astype(o_ref.dtype)

    @pl.when(c == pl.num_programs(1) - 1)
    def _():
        s_ref[0] = st_ref[...]


def _ret_prompt(proj, norm_g, chunk=256):
    nc = SEQ // chunk
    decay, q_dec, k_dec, c_dec = _ret_tables(chunk)
    ret_inv = 1.0 / (ROPE_THETA ** jnp.linspace(0.0, 1.0, RET_KEY_DIM // 2, dtype=F32))
    cos, sin = _rope_tables(jnp.arange(SEQ, dtype=jnp.int32), ret_inv)
    col_blk = lambda j: pl.BlockSpec((chunk, 512), lambda b, c: (b * nc + c, j))
    const = lambda shape: pl.BlockSpec(shape, lambda b, c: (0,) * len(shape))
    return pl.pallas_call(
        functools.partial(_ret_prompt_kernel, c_dec=c_dec), grid=(BATCH, nc),
        in_specs=[col_blk(0), col_blk(1), col_blk(2),
                  pl.BlockSpec((chunk, 128), lambda b, c: (c, 0)), pl.BlockSpec((chunk, 128), lambda b, c: (c, 0)),
                  const((RET_HEADS, chunk, chunk)), const((RET_HEADS, chunk, 1)), const((RET_HEADS, chunk, 1)),
                  const((1, BRANCH_WIDTH))],
        out_specs=[pl.BlockSpec((chunk, BRANCH_WIDTH), lambda b, c: (b * nc + c, 0)),
                   pl.BlockSpec((1, RET_HEADS, RET_KEY_DIM, RET_VAL_DIM), lambda b, c: (b, 0, 0, 0))],
        out_shape=[jax.ShapeDtypeStruct((BATCH * SEQ, BRANCH_WIDTH), BF16),
                   jax.ShapeDtypeStruct((BATCH, RET_HEADS, RET_KEY_DIM, RET_VAL_DIM), F32)],
        scratch_shapes=[pltpu.VMEM((RET_HEADS, RET_KEY_DIM, RET_VAL_DIM), F32)],
        compiler_params=_cparams(2), name="ret_prompt")(
            proj, proj, proj, cos, sin, decay, q_dec, jnp.swapaxes(k_dec, 1, 2), norm_g.reshape(1, BRANCH_WIDTH))


GLA_C = 64
GLA_SUB = 16


def _gla_chunk(q, k, v, la, state_t):
    c = GLA_C
    row = lax.broadcasted_iota(jnp.int32, (c, c), 0)
    col = lax.broadcasted_iota(jnp.int32, (c, c), 1)
    tril = (col <= row).astype(BF16)
    la_hi = la.astype(BF16)
    r1 = la - la_hi.astype(F32)
    la_mid = r1.astype(BF16)
    la_lo = (r1 - la_mid.astype(F32)).astype(BF16)
    b = _dot(tril, la_hi) + _dot(tril, la_mid) + _dot(tril, la_lo)
    b_last = b[c - 1:c, :]
    v_bf = v.astype(BF16)
    o = _dot_nt((q * jnp.exp(b)).astype(BF16), state_t.astype(BF16))
    k_state = (k * jnp.exp(b_last - b)).astype(BF16)
    new_state_t = state_t * jnp.exp(b_last) + _dot_tn(v_bf, k_state)

    rsub = lax.broadcasted_iota(jnp.int32, (c, GLA_KEY_DIM), 0)
    tri3 = (lax.broadcasted_iota(jnp.int32, (GLA_SUB, GLA_SUB, GLA_KEY_DIM), 1)
            <= lax.broadcasted_iota(jnp.int32, (GLA_SUB, GLA_SUB, GLA_KEY_DIM), 0))
    o_rows = []
    for blk in range(c // GLA_SUB):
        r0 = blk * GLA_SUB
        qi, ki, bi = q[r0:r0 + GLA_SUB], k[r0:r0 + GLA_SUB], b[r0:r0 + GLA_SUB]
        rel = jnp.where(tri3, jnp.exp(jnp.minimum(bi[:, None, :] - bi[None, :, :], 0.0)), 0.0)
        att_d = jnp.sum(qi[:, None, :] * ki[None, :, :] * rel, axis=-1)
        o_blk = o[r0:r0 + GLA_SUB] + _dot(att_d.astype(BF16), v_bf[r0:r0 + GLA_SUB])
        if blk > 0:
            ref_b = b[r0 - 1:r0, :]
            q_off = (qi * jnp.exp(bi - ref_b)).astype(BF16)
            k_off = jnp.where(rsub < r0, k * jnp.exp(jnp.minimum(ref_b - b, 0.0)), 0.0).astype(BF16)
            o_blk = o_blk + _dot(_dot_nt(q_off, k_off).astype(BF16), v_bf)
        o_rows.append(o_blk)
    return jnp.concatenate(o_rows, axis=0), new_state_t


def _log_sigmoid(x):
    return jnp.minimum(x, 0.0) - jnp.log1p(jnp.exp(-jnp.abs(x)))


def _gla_prompt_kernel(qk_ref, v_ref, ga_ref, r_ref, wa_ref, ba_ref, ng_ref, o_ref, s_ref, st_ref, la_ref, *, n_chunks):
    step = pl.program_id(1)

    @pl.when(step == 0)
    def _():
        st_ref[...] = jnp.zeros_like(st_ref)

    la_ref[...] = _log_sigmoid(_dot(ga_ref[...].astype(BF16), wa_ref[...]) + ba_ref[...]) * (1.0 / GLA_GATE_TAU)

    def chunk_body(ci, carry):
        r0 = pl.multiple_of(ci * GLA_C, GLA_C)
        rows = pl.ds(r0, GLA_C)
        for h in range(GLA_HEADS):
            kcols = slice(h * GLA_KEY_DIM, (h + 1) * GLA_KEY_DIM)
            vcols = slice(h * GLA_VAL_DIM, (h + 1) * GLA_VAL_DIM)
            q = qk_ref[rows, kcols] * (GLA_KEY_DIM ** -0.5)
            k = qk_ref[rows, GLA_HEADS * GLA_KEY_DIM + h * GLA_KEY_DIM:GLA_HEADS * GLA_KEY_DIM + (h + 1) * GLA_KEY_DIM]
            o, new_state = _gla_chunk(q, k, v_ref[rows, vcols], la_ref[rows, kcols], st_ref[h])
            st_ref[h] = new_state
            o_ref[rows, vcols] = _head_rms_gate(o, ng_ref[:, vcols], r_ref[rows, vcols]).astype(o_ref.dtype)
        return carry

    lax.fori_loop(0, n_chunks, chunk_body, 0)

    @pl.when(step == pl.num_programs(1) - 1)
    def _():
        s_ref[0] = st_ref[...]


def _gla_prompt(proj, w_a, b_a, norm_g, tile=256):
    nt = SEQ // tile
    width = GLA_HEADS * GLA_KEY_DIM
    w_pad = jnp.zeros((128, width), F32).at[:GLA_GATE_RANK].set(w_a.reshape(GLA_GATE_RANK, width)).astype(BF16)
    col_blk = lambda w, j: pl.BlockSpec((tile, w), lambda b, c: (b * nt + c, j))
    const = lambda shape: pl.BlockSpec(shape, lambda b, c: (0,) * len(shape))
    return pl.pallas_call(
        functools.partial(_gla_prompt_kernel, n_chunks=tile // GLA_C), grid=(BATCH, nt),
        in_specs=[col_blk(512, 3072 // 512), col_blk(512, 3584 // 512), col_blk(128, SLAB_GA // 128), col_blk(512, SLAB_GR // 512),
                  const((128, width)), const((1, width)), const((1, BRANCH_WIDTH))],
        out_specs=[pl.BlockSpec((tile, BRANCH_WIDTH), lambda b, c: (b * nt + c, 0)),
                   pl.BlockSpec((1, GLA_HEADS, GLA_VAL_DIM, GLA_KEY_DIM), lambda b, c: (b, 0, 0, 0))],
        out_shape=[jax.ShapeDtypeStruct((BATCH * SEQ, BRANCH_WIDTH), BF16),
                   jax.ShapeDtypeStruct((BATCH, GLA_HEADS, GLA_VAL_DIM, GLA_KEY_DIM), F32)],
        scratch_shapes=[pltpu.VMEM((GLA_HEADS, GLA_VAL_DIM, GLA_KEY_DIM), F32), pltpu.VMEM((tile, width), F32)],
        compiler_params=_cparams(2), name="gla_prompt")(
            proj, proj, proj, proj, w_pad, b_a.reshape(1, width), norm_g.reshape(1, BRANCH_WIDTH))


SAMPLE_NB = 8


def _ret_sample_kernel(q_ref, kt_ref, v_ref, s0_ref, dec_ref, qd_ref, kd_ref, cd_ref, o_ref, s_ref):
    nb = SAMPLE_NB * RET_HEADS
    q = q_ref[...].reshape(nb, DEC_SEQ, RET_KEY_DIM)
    kt = kt_ref[...].reshape(nb, RET_KEY_DIM, DEC_SEQ)
    v = v_ref[...].reshape(nb, DEC_SEQ, RET_VAL_DIM).astype(BF16)
    s0 = s0_ref[...].reshape(nb, RET_KEY_DIM, RET_VAL_DIM)
    att = jnp.einsum('bqd,bdk->bqk', q.astype(BF16), kt.astype(BF16), preferred_element_type=F32) * dec_ref[...]
    o = (jnp.einsum('bqk,bkv->bqv', att.astype(BF16), v, preferred_element_type=F32)
         + jnp.einsum('bqd,bdv->bqv', (q * qd_ref[...]).astype(BF16), s0.astype(BF16), preferred_element_type=F32))
    s_new = s0 * cd_ref[...] + jnp.einsum('bdt,btv->bdv', (kt * kd_ref[...]).astype(BF16), v, preferred_element_type=F32)
    o_ref[...] = o.reshape(SAMPLE_NB, RET_HEADS, DEC_SEQ, RET_VAL_DIM)
    s_ref[...] = s_new.reshape(SAMPLE_NB, RET_HEADS, RET_KEY_DIM, RET_VAL_DIM)


def _ret_sample(q, kt, v, s0):
    decay, q_dec, k_dec, c_dec = _ret_tables(DEC_SEQ)
    rep = lambda t: jnp.tile(t, (SAMPLE_NB,) + (1,) * (t.ndim - 1))
    cd = rep(jnp.asarray(c_dec, F32).reshape(RET_HEADS, 1, 1))
    blk = lambda *dims: pl.BlockSpec((SAMPLE_NB, RET_HEADS) + dims, lambda i: (i, 0, 0, 0))
    full = lambda t: pl.BlockSpec(t.shape, lambda i: (0, 0, 0))
    tabs = [rep(decay), rep(q_dec), rep(k_dec), cd]
    return pl.pallas_call(
        _ret_sample_kernel, grid=(DEC_BATCH // SAMPLE_NB,),
        in_specs=[blk(DEC_SEQ, RET_KEY_DIM), blk(RET_KEY_DIM, DEC_SEQ), blk(DEC_SEQ, RET_VAL_DIM),
                  blk(RET_KEY_DIM, RET_VAL_DIM)] + [full(t) for t in tabs],
        out_specs=[blk(DEC_SEQ, RET_VAL_DIM), blk(RET_KEY_DIM, RET_VAL_DIM)],
        out_shape=[jax.ShapeDtypeStruct((DEC_BATCH, RET_HEADS, DEC_SEQ, RET_VAL_DIM), F32),
                   jax.ShapeDtypeStruct((DEC_BATCH, RET_HEADS, RET_KEY_DIM, RET_VAL_DIM), F32)],
        compiler_params=_cparams(1), name="ret_sample")(q, kt, v, s0, *tabs)


def _gla_sample_kernel(q_ref, k_ref, kt_ref, v_ref, la_ref, lat_ref, s0_ref, o_ref, s_ref):
    nb = SAMPLE_NB * GLA_HEADS
    t_n = DEC_SEQ
    q = q_ref[...].reshape(nb, t_n, GLA_KEY_DIM)
    k = k_ref[...].reshape(nb, t_n, GLA_KEY_DIM)
    v = v_ref[...].reshape(nb, t_n, GLA_VAL_DIM)
    la = la_ref[...].reshape(nb, t_n, GLA_KEY_DIM)
    kt = kt_ref[...].reshape(nb, GLA_KEY_DIM, t_n)
    lat = lat_ref[...].reshape(nb, GLA_KEY_DIM, t_n)
    s0 = s0_ref[...].reshape(nb, GLA_KEY_DIM, GLA_VAL_DIM)
    b_rows, bt_cols = [], []
    run, run_t = None, None
    for t in range(t_n):
        run = la[:, t, :] if run is None else run + la[:, t, :]
        run_t = lat[:, :, t:t + 1] if run_t is None else run_t + lat[:, :, t:t + 1]
        b_rows.append(run)
        bt_cols.append(run_t)
    b_last_t = bt_cols[-1]
    q_state = jnp.stack([q[:, t, :] * jnp.exp(b_rows[t]) for t in range(t_n)], axis=1)
    o = jnp.einsum('bqd,bdv->bqv', q_state.astype(BF16), s0.astype(BF16), preferred_element_type=F32)
    o_rows = []
    for i in range(t_n):
        o_i = o[:, i, :]
        for j in range(i + 1):
            w = jnp.sum(q[:, i, :] * k[:, j, :] * jnp.exp(jnp.minimum(b_rows[i] - b_rows[j], 0.0)),
                        axis=-1, keepdims=True)
            o_i = o_i + w * v[:, j, :]
        o_rows.append(o_i)
    k_state_t = jnp.concatenate([kt[:, :, t:t + 1] * jnp.exp(b_last_t - bt_cols[t]) for t in range(t_n)], axis=2)
    s_new = s0 * jnp.exp(b_last_t) + jnp.einsum('bdt,btv->bdv', k_state_t.astype(BF16), v.astype(BF16),
                                                  preferred_element_type=F32)
    o_ref[...] = jnp.stack(o_rows, axis=1).reshape(SAMPLE_NB, GLA_HEADS, t_n, GLA_VAL_DIM)
    s_ref[...] = s_new.reshape(SAMPLE_NB, GLA_HEADS, GLA_KEY_DIM, GLA_VAL_DIM)


def _gla_sample(q, k, v, la, s0):
    kt = jnp.swapaxes(k, 2, 3)
    lat = jnp.swapaxes(la, 2, 3)
    blk = lambda *dims: pl.BlockSpec((SAMPLE_NB, GLA_HEADS) + dims, lambda i: (i, 0, 0, 0))
    return pl.pallas_call(
        _gla_sample_kernel, grid=(DEC_BATCH // SAMPLE_NB,),
        in_specs=[blk(DEC_SEQ, GLA_KEY_DIM), blk(DEC_SEQ, GLA_KEY_DIM), blk(GLA_KEY_DIM, DEC_SEQ),
                  blk(DEC_SEQ, GLA_VAL_DIM), blk(DEC_SEQ, GLA_KEY_DIM), blk(GLA_KEY_DIM, DEC_SEQ),
                  blk(GLA_KEY_DIM, GLA_VAL_DIM)],
        out_specs=[blk(DEC_SEQ, GLA_VAL_DIM), blk(GLA_KEY_DIM, GLA_VAL_DIM)],
        out_shape=[jax.ShapeDtypeStruct((DEC_BATCH, GLA_HEADS, DEC_SEQ, GLA_VAL_DIM), F32),
                   jax.ShapeDtypeStruct((DEC_BATCH, GLA_HEADS, GLA_KEY_DIM, GLA_VAL_DIM), F32)],
        compiler_params=_cparams(1), name="gla_sample")(q, k, kt, v, la, lat, s0)


Q_ROWS = N_HEADS * DEC_SEQ
MLA_PAGES = 64
MLA_KEYS = MLA_PAGES * PAGE_SIZE
MLA_STEPS = N_PAGES // MLA_PAGES


def _new_token_mask():
    tok = lax.broadcasted_iota(jnp.int32, (Q_ROWS, DEC_SEQ), 0) & (DEC_SEQ - 1)
    key = lax.broadcasted_iota(jnp.int32, (Q_ROWS, DEC_SEQ), 1)
    return key <= tok


def _mla_sample_kernel(pt_ref, ql_ref, qr_ref, cn_ref, kn_ref, lat_hbm, rope_hbm, o_ref,
                       lat_buf, rope_buf, lat_sem, rope_sem, m_ref, l_ref, acc_ref, *, layer):
    s = pl.program_id(0)
    c = pl.program_id(1)
    g = s * MLA_STEPS + c
    slot = lax.rem(g, 2)

    def copies(seq, chunk, slot_):
        out = []
        for i in range(MLA_PAGES):
            page = pt_ref[seq, chunk * MLA_PAGES + i]
            out.append(pltpu.make_async_copy(lat_hbm.at[layer, page], lat_buf.at[slot_, pl.ds(i * PAGE_SIZE, PAGE_SIZE)],
                                             lat_sem.at[slot_]))
            out.append(pltpu.make_async_copy(rope_hbm.at[layer, page], rope_buf.at[slot_, i], rope_sem.at[slot_]))
        return out

    @pl.when(g == 0)
    def _():
        for cp in copies(0, 0, 0):
            cp.start()

    @pl.when(g + 1 < DEC_BATCH * MLA_STEPS)
    def _():
        nxt = g + 1
        for cp in copies(nxt // MLA_STEPS, lax.rem(nxt, MLA_STEPS), 1 - slot):
            cp.start()

    for cp in copies(s, c, slot):
        cp.wait()

    @pl.when(c == 0)
    def _():
        m_ref[...] = jnp.full_like(m_ref, NEG_INF)
        l_ref[...] = jnp.zeros_like(l_ref)
        acc_ref[...] = jnp.zeros_like(acc_ref)

    ql = ql_ref[0].astype(BF16)
    qr = qr_ref[0].astype(BF16)
    lat = lat_buf[slot].astype(BF16)
    s_rope = jnp.concatenate([_dot(qr, rope_buf[slot, i].astype(BF16)) for i in range(MLA_PAGES)], axis=1)
    sc = (_dot_nt(ql, lat) + s_rope) * MLA_SCALE
    _softmax_step(sc, m_ref, l_ref, acc_ref, lat)

    @pl.when(c == MLA_STEPS - 1)
    def _():
        cn = cn_ref[0].astype(BF16)
        sn = (_dot_nt(ql, cn) + _dot_nt(qr, kn_ref[0].astype(BF16))) * MLA_SCALE
        sn = jnp.where(_new_token_mask(), sn, NEG_INF)
        _softmax_step(sn, m_ref, l_ref, acc_ref, cn)
        o_ref[0] = acc_ref[...] / l_ref[...]


def _mla_sample(page_table, ql, qr, cn, kn, cache_lat, cache_rope_t, layer):
    seq_blk = lambda r, d: pl.BlockSpec((1, r, d), lambda s, c, pt: (s, 0, 0))
    grid_spec = pltpu.PrefetchScalarGridSpec(
        num_scalar_prefetch=1, grid=(DEC_BATCH, MLA_STEPS),
        in_specs=[seq_blk(Q_ROWS, MLA_KV_RANK), seq_blk(Q_ROWS, MLA_ROPE_DIM), seq_blk(DEC_SEQ, MLA_KV_RANK),
                  seq_blk(DEC_SEQ, MLA_ROPE_DIM), pl.BlockSpec(memory_space=pl.ANY), pl.BlockSpec(memory_space=pl.ANY)],
        out_specs=seq_blk(Q_ROWS, MLA_KV_RANK),
        scratch_shapes=[pltpu.VMEM((2, MLA_KEYS, MLA_KV_RANK), F32),
                        pltpu.VMEM((2, MLA_PAGES, MLA_ROPE_DIM, PAGE_SIZE), F32),
                        pltpu.SemaphoreType.DMA((2,)), pltpu.SemaphoreType.DMA((2,)),
                        pltpu.VMEM((Q_ROWS, 1), F32), pltpu.VMEM((Q_ROWS, 1), F32), pltpu.VMEM((Q_ROWS, MLA_KV_RANK), F32)])
    return pl.pallas_call(
        functools.partial(_mla_sample_kernel, layer=layer), grid_spec=grid_spec,
        out_shape=jax.ShapeDtypeStruct((DEC_BATCH, Q_ROWS, MLA_KV_RANK), F32),
        compiler_params=_cparams(2), name="mla_sample")(page_table, ql, qr, cn, kn, cache_lat, cache_rope_t)


N_MOBA_BLOCKS = PAST_LEN // MOBA_BLOCK
MOBA_CHUNK = 4096
MOBA_CHUNKS = PAST_LEN // MOBA_CHUNK
BLOCKS_PER_CHUNK = MOBA_CHUNK // MOBA_BLOCK


def _moba_sample_kernel(pt_ref, q_ref, kn_ref, vn_ref, k_hbm, v_hbm, o_ref, k_buf, v_buf, k_sem, v_sem, s_ref, p_ref,
                        *, layer):
    s = pl.program_id(0)
    slot = lax.rem(s, 2)

    def copies(hbm, buf, sem, seq, slot_):
        return [pltpu.make_async_copy(hbm.at[layer, pt_ref[seq, i]], buf.at[slot_, pl.ds(i * PAGE_SIZE, PAGE_SIZE)],
                                      sem.at[slot_]) for i in range(N_PAGES)]

    def start(seq, slot_):
        for cp in copies(k_hbm, k_buf, k_sem, seq, slot_):
            cp.start()
        for cp in copies(v_hbm, v_buf, v_sem, seq, slot_):
            cp.start()

    @pl.when(s == 0)
    def _():
        start(0, 0)

    @pl.when(s + 1 < DEC_BATCH)
    def _():
        start(s + 1, 1 - slot)

    for cp in copies(k_hbm, k_buf, k_sem, s, slot):
        cp.wait()

    qb = q_ref[0].astype(BF16)
    lane = lax.broadcasted_iota(jnp.int32, (Q_ROWS, 128), 1)
    gate = jnp.zeros((Q_ROWS, 128), F32)
    for ci in range(MOBA_CHUNKS):
        raw = _dot_nt(qb, k_buf[slot, ci * MOBA_CHUNK:(ci + 1) * MOBA_CHUNK, :].astype(BF16))
        s_ref[ci] = raw
        for blk in range(BLOCKS_PER_CHUNK):
            tot = jnp.sum(raw[:, blk * MOBA_BLOCK:(blk + 1) * MOBA_BLOCK], axis=-1, keepdims=True)
            gate = jnp.where(lane == ci * BLOCKS_PER_CHUNK + blk, tot * (1.0 / MOBA_BLOCK), gate)

    sel = _topk_select(gate, N_MOBA_BLOCKS, N_MOBA_BLOCKS).astype(BF16)
    sn = jnp.where(_new_token_mask(), _dot_nt(qb, kn_ref[0].astype(BF16)) * MOBA_SCALE, NEG_INF)
    m = jnp.max(sn, axis=-1, keepdims=True)
    key_blk = lax.broadcasted_iota(jnp.int32, (128, MOBA_CHUNK), 1) // MOBA_BLOCK
    blk_row = lax.broadcasted_iota(jnp.int32, (128, MOBA_CHUNK), 0)
    masked = []
    for ci in range(MOBA_CHUNKS):
        expand = (blk_row == key_blk + ci * BLOCKS_PER_CHUNK).astype(BF16)
        picked = _dot(sel, expand) > 0.5
        sc = jnp.where(picked, s_ref[ci] * MOBA_SCALE, NEG_INF)
        masked.append(sc)
        m = jnp.maximum(m, jnp.max(sc, axis=-1, keepdims=True))
    pn = jnp.exp(sn - m)
    l = jnp.sum(pn, axis=-1, keepdims=True)
    for ci in range(MOBA_CHUNKS):
        p = jnp.exp(masked[ci] - m)
        l = l + jnp.sum(p, axis=-1, keepdims=True)
        p_ref[ci] = p.astype(BF16)

    for cp in copies(v_hbm, v_buf, v_sem, s, slot):
        cp.wait()

    acc = _dot(pn.astype(BF16), vn_ref[0].astype(BF16))
    for ci in range(MOBA_CHUNKS):
        acc = acc + _dot(p_ref[ci], v_buf[slot, ci * MOBA_CHUNK:(ci + 1) * MOBA_CHUNK, :].astype(BF16))
    o_ref[0] = acc / l


def _moba_sample(page_table, q, kn, vn, cache_k, cache_v, layer):
    seq_blk = lambda r, d: pl.BlockSpec((1, r, d), lambda s, pt: (s, 0, 0))
    grid_spec = pltpu.PrefetchScalarGridSpec(
        num_scalar_prefetch=1, grid=(DEC_BATCH,),
        in_specs=[seq_blk(Q_ROWS, MOBA_HEAD_DIM), seq_blk(DEC_SEQ, MOBA_HEAD_DIM), seq_blk(DEC_SEQ, MOBA_HEAD_DIM),
                  pl.BlockSpec(memory_space=pl.ANY), pl.BlockSpec(memory_space=pl.ANY)],
        out_specs=seq_blk(Q_ROWS, MOBA_HEAD_DIM),
        scratch_shapes=[pltpu.VMEM((2, PAST_LEN, MOBA_HEAD_DIM), F32), pltpu.VMEM((2, PAST_LEN, MOBA_HEAD_DIM), F32),
                        pltpu.SemaphoreType.DMA((2,)), pltpu.SemaphoreType.DMA((2,)),
                        pltpu.VMEM((MOBA_CHUNKS, Q_ROWS, MOBA_CHUNK), F32),
                        pltpu.VMEM((MOBA_CHUNKS, Q_ROWS, MOBA_CHUNK), BF16)])
    return pl.pallas_call(
        functools.partial(_moba_sample_kernel, layer=layer), grid_spec=grid_spec,
        out_shape=jax.ShapeDtypeStruct((DEC_BATCH, Q_ROWS, MOBA_HEAD_DIM), F32),
        compiler_params=_cparams(1), name="moba_sample")(page_table, q, kn, vn, cache_k, cache_v)


def _rms(x, g):
    return x * lax.rsqrt(jnp.mean(x * x, -1, keepdims=True) + RMS_EPS) * g


def _rope(x, pos, inv_freq):
    ang = pos.astype(F32)[:, None] * inv_freq[None, :]
    shape = (x.shape[0],) + (1,) * (x.ndim - 2) + (inv_freq.shape[0],)
    cos = jnp.cos(ang).reshape(shape)
    sin = jnp.sin(ang).reshape(shape)
    x1, x2 = jnp.split(x, 2, axis=-1)
    return jnp.concatenate([x1 * cos - x2 * sin, x2 * cos + x1 * sin], axis=-1)


def _layer_weights(l, W):
    w_uq = W['mla_w_uq'][l]
    return dict(
        w_in_t=jnp.swapaxes(W['w_in'], 1, 2),
        b_gate=W['b_gate'][l].reshape(1, N_BRANCH * D_MODEL),
        w_uq=jnp.concatenate([w_uq[:, :, :MLA_NOPE_DIM].reshape(MLA_Q_RANK, -1),
                              w_uq[:, :, MLA_NOPE_DIM:].reshape(MLA_Q_RANK, -1)], axis=1).astype(BF16),
        w_uk_t=jnp.transpose(W['mla_w_uk'][l], (1, 2, 0)).astype(BF16),
        w_uv=jnp.transpose(W['mla_w_uv'][l], (1, 0, 2)).astype(BF16),
    )


def _mla_prep_kernel(cq_ref, kv_ref, qg_ref, kg_ref, wuq_ref, wuk_ref, cos_ref, sin_ref,
                     ql_ref, qr_ref, c_ref, cbf_ref, kr_ref, krbf_ref):
    cq = cq_ref[...]
    cqn = cq * lax.rsqrt(jnp.mean(cq * cq, axis=-1, keepdims=True) + RMS_EPS) * qg_ref[...]
    qm = _dot(cqn.astype(BF16), wuq_ref[...])
    cos, sin = cos_ref[...], sin_ref[...]
    for h in range(MLA_HEADS):
        ql_ref[h] = _dot(qm[:, h * MLA_NOPE_DIM:(h + 1) * MLA_NOPE_DIM].astype(BF16), wuk_ref[h]).astype(BF16)
    rope0 = MLA_HEADS * MLA_NOPE_DIM
    for j in range(MLA_HEADS // 2):
        pair = _rope128(qm[:, rope0 + j * 128:rope0 + (j + 1) * 128], cos, sin)
        qr_ref[2 * j] = pair[:, :MLA_ROPE_DIM].astype(BF16)
        qr_ref[2 * j + 1] = pair[:, MLA_ROPE_DIM:].astype(BF16)
    ckv = kv_ref[:, :MLA_KV_RANK]
    c = ckv * lax.rsqrt(jnp.mean(ckv * ckv, axis=-1, keepdims=True) + RMS_EPS) * kg_ref[...]
    c_ref[...] = c
    cbf_ref[...] = c.astype(BF16)
    kr = _rope128(kv_ref[:, MLA_KV_RANK:], cos, sin)[:, :MLA_ROPE_DIM]
    kr_ref[...] = kr
    krbf_ref[...] = kr.astype(BF16)


def _mla_prep(proj, pos, l, W, LW, tm=512):
    m = proj.shape[0]
    mla_inv = 1.0 / (ROPE_THETA ** (jnp.arange(0, MLA_ROPE_DIM, 2, dtype=F32) / MLA_ROPE_DIM))
    cos, sin = _rope_tables(pos, mla_inv)
    n_tab = cos.shape[0] // tm
    const = lambda shape: pl.BlockSpec(shape, lambda i: (0,) * len(shape))
    tab = pl.BlockSpec((tm, 128), lambda i: (i % n_tab, 0))
    width = MLA_HEADS * (MLA_NOPE_DIM + MLA_ROPE_DIM)
    return pl.pallas_call(
        _mla_prep_kernel, grid=(m // tm,),
        in_specs=[pl.BlockSpec((tm, MLA_Q_RANK), lambda i: (i, 2304 // MLA_Q_RANK)),
                  pl.BlockSpec((tm, 384), lambda i: (i, 2688 // 384)),
                  const((1, MLA_Q_RANK)), const((1, MLA_KV_RANK)), const((MLA_Q_RANK, width)),
                  const((MLA_HEADS, MLA_NOPE_DIM, MLA_KV_RANK)), tab, tab],
        out_specs=[pl.BlockSpec((MLA_HEADS, tm, MLA_KV_RANK), lambda i: (0, i, 0)),
                   pl.BlockSpec((MLA_HEADS, tm, MLA_ROPE_DIM), lambda i: (0, i, 0)),
                   pl.BlockSpec((tm, MLA_KV_RANK), lambda i: (i, 0)), pl.BlockSpec((tm, MLA_KV_RANK), lambda i: (i, 0)),
                   pl.BlockSpec((tm, MLA_ROPE_DIM), lambda i: (i, 0)), pl.BlockSpec((tm, MLA_ROPE_DIM), lambda i: (i, 0))],
        out_shape=[jax.ShapeDtypeStruct((MLA_HEADS, m, MLA_KV_RANK), BF16),
                   jax.ShapeDtypeStruct((MLA_HEADS, m, MLA_ROPE_DIM), BF16),
                   jax.ShapeDtypeStruct((m, MLA_KV_RANK), F32), jax.ShapeDtypeStruct((m, MLA_KV_RANK), BF16),
                   jax.ShapeDtypeStruct((m, MLA_ROPE_DIM), F32), jax.ShapeDtypeStruct((m, MLA_ROPE_DIM), BF16)],
        compiler_params=_cparams(1), name="mla_prep")(
            proj, proj, W['mla_q_norm_g'][l].reshape(1, -1), W['mla_kv_norm_g'][l].reshape(1, -1),
            LW['w_uq'], LW['w_uk_t'], cos, sin)


def _project(x_bf, pos, l, W, LW, tm):
    proj = _proj(x_bf, LW['w_in_t'], l, tm)
    p = {'proj': proj, 'moba_k': proj[:, 2048:2176], 'moba_v': proj[:, 2176:2304]}
    (p['mla_q_lat'], p['mla_q_rope'], p['mla_c'], p['mla_c_bf'], p['mla_kr'], p['mla_kr_bf']) = _mla_prep(proj, pos, l, W, LW)
    return p


def _sample_mixer_inputs(proj, pos, l, W):
    m = proj.shape[0]
    ret_inv = 1.0 / (ROPE_THETA ** jnp.linspace(0.0, 1.0, RET_KEY_DIM // 2, dtype=F32))
    p = {}
    p['ret_q'] = _rope(proj[:, 0:256].reshape(m, RET_HEADS, RET_KEY_DIM), pos, ret_inv)
    p['ret_k'] = _rope(proj[:, 256:512].reshape(m, RET_HEADS, RET_KEY_DIM), pos, ret_inv) * (RET_KEY_DIM ** -0.5)
    p['ret_v'] = proj[:, 512:1024].reshape(m, RET_HEADS, RET_VAL_DIM)
    p['ret_g'] = proj[:, 1024:1536]
    p['moba_q'] = proj[:, 1536:2048]
    ga = proj[:, SLAB_GA:SLAB_GA + GLA_GATE_RANK]
    p['gla_q'] = proj[:, 3072:3328].reshape(m, GLA_HEADS, GLA_KEY_DIM) * (GLA_KEY_DIM ** -0.5)
    p['gla_k'] = proj[:, 3328:3584].reshape(m, GLA_HEADS, GLA_KEY_DIM)
    p['gla_v'] = proj[:, 3584:4096].reshape(m, GLA_HEADS, GLA_VAL_DIM)
    a_logit = _matmul(ga.astype(BF16), W['gla_w_a'][l].reshape(GLA_GATE_RANK, -1).astype(BF16), m, 256, F32)
    a_logit = a_logit.reshape(m, GLA_HEADS, GLA_KEY_DIM) + W['gla_b_a'][l]
    p['gla_log_a'] = jax.nn.log_sigmoid(a_logit) / GLA_GATE_TAU
    p['gla_r'] = proj[:, SLAB_GR:SLAB_GR + BRANCH_WIDTH]
    return p


def _finish_layer(x, x_bf, l, W, LW, branch_list, tm):
    branches = jnp.stack(branch_list, axis=0)
    merged = _merge(x_bf, LW['w_in_t'], LW['b_gate'], branches, W['w_branch'], l, tm, 512)
    x1, x1_bf = _mm_res_ln(merged, W['w_out'], l, x, W['ln1_g'][l], W['ln1_b'][l], 512, 512)
    hidden = _ffn_up(x1_bf, W['ffn_w_gate'], W['ffn_w_up'], l, tm, 512)
    return _mm_res_ln(hidden, W['ffn_w_down'], l, x1, W['ln2_g'][l], W['ln2_b'][l], 512, 512)


def _prompt_layer(x, x_bf, l, W, LW):
    m = BATCH * SEQ
    p = _project(x_bf, jnp.arange(SEQ, dtype=jnp.int32), l, W, LW, 1024)
    ret_o, ret_s = _ret_prompt(p['proj'], W['ret_norm_g'][l])
    gla_o, gla_st = _gla_prompt(p['proj'], W['gla_w_a'][l], W['gla_b_a'][l], W['gla_norm_g'][l])
    moba_o = _moba_prompt(p['proj'])
    mla_o = _mla_prompt(p['mla_q_lat'], p['mla_q_rope'], p['mla_c_bf'], p['mla_kr_bf'], LW['w_uv'])
    x, x_bf = _finish_layer(x, x_bf, l, W, LW, [ret_o, moba_o, mla_o, gla_o], 1024)
    new = (p['moba_k'].reshape(BATCH, SEQ, 1, MOBA_HEAD_DIM), p['moba_v'].reshape(BATCH, SEQ, 1, MOBA_HEAD_DIM),
           p['mla_c'].reshape(BATCH, SEQ, MLA_KV_RANK), p['mla_kr'].reshape(BATCH, SEQ, MLA_ROPE_DIM),
           ret_s, jnp.swapaxes(gla_st, 2, 3))
    return x, x_bf, new


def _sample_layer(x, x_bf, l, W, LW, caches, page_table):
    m = DEC_BATCH * DEC_SEQ
    pos = jnp.tile(PAST_LEN + jnp.arange(DEC_SEQ, dtype=jnp.int32), DEC_BATCH)
    p = _project(x_bf, pos, l, W, LW, 512)
    p.update(_sample_mixer_inputs(p['proj'], pos, l, W))
    seq_heads = lambda t, d: jnp.swapaxes(t.reshape(DEC_BATCH, DEC_SEQ, N_HEADS, d), 1, 2)
    back = lambda t, d: jnp.swapaxes(t, 1, 2).reshape(m, N_HEADS * d)

    ret_k = seq_heads(p['ret_k'], RET_KEY_DIM)
    ret_o, ret_s = _ret_sample(seq_heads(p['ret_q'], RET_KEY_DIM), jnp.swapaxes(ret_k, 2, 3),
                               seq_heads(p['ret_v'], RET_VAL_DIM), caches['state_retention'][l])
    gla_o, gla_s = _gla_sample(seq_heads(p['gla_q'], GLA_KEY_DIM), seq_heads(p['gla_k'], GLA_KEY_DIM),
                               seq_heads(p['gla_v'], GLA_VAL_DIM), seq_heads(p['gla_log_a'], GLA_KEY_DIM),
                               caches['state_gla'][l])
    moba_q = seq_heads(p['moba_q'], MOBA_HEAD_DIM).reshape(DEC_BATCH, Q_ROWS, MOBA_HEAD_DIM)
    moba_o = _moba_sample(page_table, moba_q, p['moba_k'].reshape(DEC_BATCH, DEC_SEQ, MOBA_HEAD_DIM),
                          p['moba_v'].reshape(DEC_BATCH, DEC_SEQ, MOBA_HEAD_DIM),
                          caches['cache_moba_k'].reshape(DEPTH, -1, PAGE_SIZE, MOBA_HEAD_DIM),
                          caches['cache_moba_v'].reshape(DEPTH, -1, PAGE_SIZE, MOBA_HEAD_DIM), l)
    moba_o = back(moba_o.reshape(DEC_BATCH, N_HEADS, DEC_SEQ, MOBA_HEAD_DIM), MOBA_HEAD_DIM)

    to_rows = lambda t, d: jnp.swapaxes(t.reshape(N_HEADS, DEC_BATCH, DEC_SEQ, d), 0, 1).reshape(DEC_BATCH, Q_ROWS, d)
    mla_lat = _mla_sample(page_table, to_rows(p['mla_q_lat'], MLA_KV_RANK), to_rows(p['mla_q_rope'], MLA_ROPE_DIM),
                          p['mla_c'].reshape(DEC_BATCH, DEC_SEQ, MLA_KV_RANK),
                          p['mla_kr'].reshape(DEC_BATCH, DEC_SEQ, MLA_ROPE_DIM),
                          caches['cache_mla_latent'], jnp.swapaxes(caches['cache_mla_rope'], 2, 3), l)
    mla_lat = jnp.swapaxes(mla_lat.reshape(DEC_BATCH, N_HEADS, DEC_SEQ, MLA_KV_RANK), 0, 1).reshape(N_HEADS, m, MLA_KV_RANK)
    mla_o = _matmul_heads(mla_lat.astype(BF16), LW['w_uv'], 512, BF16)
    mla_o = jnp.swapaxes(mla_o, 0, 1).reshape(m, BRANCH_WIDTH)

    ret = (_rms(jnp.swapaxes(ret_o, 1, 2), W['ret_norm_g'][l]).reshape(m, BRANCH_WIDTH) * jax.nn.silu(p['ret_g']))
    gla = (_rms(jnp.swapaxes(gla_o, 1, 2), W['gla_norm_g'][l]).reshape(m, BRANCH_WIDTH) * jax.nn.silu(p['gla_r']))
    x, x_bf = _finish_layer(x, x_bf, l, W, LW, [ret.astype(BF16), moba_o.astype(BF16), mla_o, gla.astype(BF16)], 512)
    new = (p['moba_k'].reshape(DEC_BATCH, DEC_SEQ, 1, MOBA_HEAD_DIM), p['moba_v'].reshape(DEC_BATCH, DEC_SEQ, 1, MOBA_HEAD_DIM),
           p['mla_c'].reshape(DEC_BATCH, DEC_SEQ, MLA_KV_RANK), p['mla_kr'].reshape(DEC_BATCH, DEC_SEQ, MLA_ROPE_DIM),
           ret_s, gla_s)
    return x, x_bf, new


def kernel(x_prompt, x_sample, cache_moba_k, cache_moba_v, cache_mla_latent, cache_mla_rope, state_retention, state_gla, page_table, w_in, b_gate, ret_norm_g, mla_q_norm_g, mla_w_uq, mla_kv_norm_g, mla_w_uk, mla_w_uv, gla_w_a, gla_b_a, gla_norm_g, w_branch, w_out, ln1_g, ln1_b, ffn_w_gate, ffn_w_up, ffn_w_down, ln2_g, ln2_b):
    W = dict(w_in=w_in, b_gate=b_gate, ret_norm_g=ret_norm_g, mla_q_norm_g=mla_q_norm_g, mla_w_uq=mla_w_uq,
             mla_kv_norm_g=mla_kv_norm_g, mla_w_uk=mla_w_uk, mla_w_uv=mla_w_uv, gla_w_a=gla_w_a, gla_b_a=gla_b_a,
             gla_norm_g=gla_norm_g, w_branch=w_branch, w_out=w_out, ln1_g=ln1_g, ln1_b=ln1_b,
             ffn_w_gate=ffn_w_gate, ffn_w_up=ffn_w_up, ffn_w_down=ffn_w_down, ln2_g=ln2_g, ln2_b=ln2_b)
    caches = dict(cache_moba_k=cache_moba_k, cache_moba_v=cache_moba_v, cache_mla_latent=cache_mla_latent,
                  cache_mla_rope=cache_mla_rope, state_retention=state_retention, state_gla=state_gla)
    xp = x_prompt.reshape(BATCH * SEQ, D_MODEL)
    xs = x_sample.reshape(DEC_BATCH * DEC_SEQ, D_MODEL)
    xp_bf, xs_bf = xp.astype(BF16), xs.astype(BF16)
    new_p, new_s = [], []
    for l in range(DEPTH):
        LW = _layer_weights(l, W)
        xp, xp_bf, st_p = _prompt_layer(xp, xp_bf, l, W, LW)
        xs, xs_bf, st_s = _sample_layer(xs, xs_bf, l, W, LW, caches, page_table)
        new_p.append(st_p)
        new_s.append(st_s)
    stack = lambda states, i: jnp.stack([s[i] for s in states], axis=0)
    return ((xp.reshape(BATCH, SEQ, D_MODEL), xs.reshape(DEC_BATCH, DEC_SEQ, D_MODEL))
            + tuple(stack(new_p, i) for i in range(6)) + tuple(stack(new_s, i) for i in range(6)))
```
